```python
import jax, jax.numpy as jnp
from jax import lax
import numpy as np

D_MODEL = 4096
BATCH = 1
SEQ = 16384
DEPTH = 4

N_MIXERS = 2
GRID_W = 64
N_MEM = 256
MEM_HEADS = 4
MEM_HEAD_DIM = D_MODEL // 16
MEM_WIDTH = MEM_HEADS * MEM_HEAD_DIM
MIXER_WIDTH = D_MODEL - MEM_WIDTH
HEAD_DIM = 128
N_Q_HEADS = MIXER_WIDTH // HEAD_DIM
N_KV_HEADS = N_Q_HEADS // 4
Q_PER_KV = N_Q_HEADS // N_KV_HEADS
KV_WIDTH = N_KV_HEADS * HEAD_DIM
Q_BLOCK = 128
ROPE_THETA = 10000.0
ROT_HALF = HEAD_DIM // 2
N_FREQ = ROT_HALF // 2
ATTN_IN = MIXER_WIDTH + 2 * KV_WIDTH + MEM_WIDTH
POOL_WINDOWS = (2, 4, 8, 16)
N_POOL_GROUPS = len(POOL_WINDOWS)
POOL_GROUP = MIXER_WIDTH // N_POOL_GROUPS
POOL_IN = MIXER_WIDTH + MEM_WIDTH
D_FF = 4 * D_MODEL
ALPHA = (2 * DEPTH) ** 0.25
BETA = (8 * DEPTH) ** -0.25
LN_EPS = 1e-5
RMS_EPS = 1e-6

kernel_name = 'hybrid_attn_pool_memory_encoder'


def layer_norm(x, g, b):
    xf = x.astype(jnp.float32)
    mu = jnp.mean(xf, axis=-1, keepdims=True)
    var = jnp.mean(jnp.square(xf - mu), axis=-1, keepdims=True)
    return ((xf - mu) * lax.rsqrt(var + LN_EPS) * g + b).astype(x.dtype)


def head_rms(x, g):
    xf = x.astype(jnp.float32)
    return (xf * lax.rsqrt(jnp.mean(jnp.square(xf), axis=-1, keepdims=True) + RMS_EPS) * g).astype(x.dtype)


def axial_rope_tables(S):
    rows = S // GRID_W
    inv_freq = ROPE_THETA ** (-jnp.arange(N_FREQ, dtype=jnp.float32) / N_FREQ)
    row_ang = jnp.arange(rows, dtype=jnp.float32)[:, None, None] * inv_freq
    col_ang = jnp.arange(GRID_W, dtype=jnp.float32)[None, :, None] * inv_freq
    ang = jnp.stack([jnp.broadcast_to(row_ang, (rows, GRID_W, N_FREQ)),
                     jnp.broadcast_to(col_ang, (rows, GRID_W, N_FREQ))], axis=2)
    ang = ang.reshape(S, 2, N_FREQ)
    return jnp.cos(ang), jnp.sin(ang)


def apply_axial_rope(x, cos, sin):
    B, S, H, D = x.shape
    xr = x.reshape(B, S, H, 2, 2, N_FREQ)
    x1, x2 = xr[..., 0, :], xr[..., 1, :]
    c, s = cos[None, :, None], sin[None, :, None]
    out = jnp.stack([x1 * c - x2 * s, x2 * c + x1 * s], axis=-2)
    return out.reshape(B, S, H, D).astype(x.dtype)


def blocked_bidirectional_gqa(q, k, v):
    B, S = q.shape[0], q.shape[1]
    n_blk = S // Q_BLOCK
    qb = jnp.moveaxis(q.reshape(B, n_blk, Q_BLOCK, N_KV_HEADS, Q_PER_KV, HEAD_DIM), 1, 0)
    scale = HEAD_DIM ** -0.5

    def one_block(q_blk):
        s = jnp.einsum('bqkgd,bskd->bkgqs', q_blk, k).astype(jnp.float32) * scale
        p = jax.nn.softmax(s, axis=-1).astype(v.dtype)
        return jnp.einsum('bkgqs,bskd->bqkgd', p, v)

    ob = lax.map(one_block, qb)
    return jnp.moveaxis(ob, 0, 1).reshape(B, S, N_Q_HEADS * HEAD_DIM)


def memory_cross_attention(qm, mem_k, mem_v):
    B, S, _ = qm.shape
    q = qm.reshape(B, S, MEM_HEADS, MEM_HEAD_DIM)
    s = jnp.einsum('bqhd,bmhd->bhqm', q, mem_k).astype(jnp.float32) * (MEM_HEAD_DIM ** -0.5)
    p = jax.nn.softmax(s, axis=-1).astype(mem_v.dtype)
    return jnp.einsum('bhqm,bmhd->bqhd', p, mem_v).reshape(B, S, MEM_WIDTH)


def centred_mean_minus_self(u, window):
    B, S, C = u.shape
    lo = window // 2
    hi = window - 1 - lo
    uf = u.astype(jnp.float32)
    cs = jnp.concatenate([jnp.zeros((B, 1, C), jnp.float32), jnp.cumsum(uf, axis=1)], axis=1)
    t = jnp.arange(S)
    start = jnp.clip(t - lo, 0, S)
    end = jnp.clip(t + hi + 1, 0, S)
    sums = jnp.take(cs, end, axis=1) - jnp.take(cs, start, axis=1)
    cnt = (end - start).astype(jnp.float32)[None, :, None]
    return (sums / cnt - uf).astype(u.dtype)


def attention_mix(h, w_in, q_gain, k_gain, w_out, mem_k, mem_v, cos, sin):
    B, S, _ = h.shape
    proj = h @ w_in
    q, k, v, qm = jnp.split(proj, [MIXER_WIDTH, MIXER_WIDTH + KV_WIDTH, MIXER_WIDTH + 2 * KV_WIDTH], axis=-1)
    q = apply_axial_rope(head_rms(q.reshape(B, S, N_Q_HEADS, HEAD_DIM), q_gain), cos, sin)
    k = apply_axial_rope(head_rms(k.reshape(B, S, N_KV_HEADS, HEAD_DIM), k_gain), cos, sin)
    v = v.reshape(B, S, N_KV_HEADS, HEAD_DIM)
    y_self = blocked_bidirectional_gqa(q, k, v)
    y_mem = memory_cross_attention(qm, mem_k, mem_v)
    return jnp.concatenate([y_self, y_mem], axis=-1) @ w_out


def pooling_mix(h, w_in, pool_w, pool_scale, w_out, mem_k, mem_v):
    B, S, _ = h.shape
    proj = h @ w_in
    u, qm = jnp.split(proj, [MIXER_WIDTH], axis=-1)
    groups = jnp.split(u, N_POOL_GROUPS, axis=-1)
    pooled = jnp.stack([centred_mean_minus_self(g, w) for g, w in zip(groups, POOL_WINDOWS)], axis=2)
    y_pool = jnp.einsum('bsgc,gcd->bsgd', pooled, pool_w).reshape(B, S, MIXER_WIDTH) * pool_scale
    y_mem = memory_cross_attention(qm, mem_k, mem_v)
    return jnp.concatenate([y_pool, y_mem], axis=-1) @ w_out


def squared_relu_mlp(h, w_ff1, w_ff2):
    return jnp.square(jax.nn.relu(h @ w_ff1)) @ w_ff2


def setup_inputs(seed: int = 0) -> dict:
    key = jax.random.key(seed)
    keys = iter(jax.random.split(key, 3 + 10 * DEPTH))

    def normal(shape, scale):
        return jax.random.normal(next(keys), shape, jnp.float32) * scale

    d = {}
    d['x'] = normal((BATCH, SEQ, D_MODEL), 1.0)
    d['mem'] = normal((BATCH, N_MEM, D_MODEL), 1.0)
    d['w_mem_kv'] = normal((D_MODEL, 2 * MEM_WIDTH), D_MODEL ** -0.5)
    for i in range(DEPTH):
        p = 'l%d_' % i
        if i % N_MIXERS == 0:
            d[p + 'w_in'] = normal((D_MODEL, ATTN_IN), D_MODEL ** -0.5)
            d[p + 'q_gain'] = 1.0 + normal((HEAD_DIM,), 0.02)
            d[p + 'k_gain'] = 1.0 + normal((HEAD_DIM,), 0.02)
        else:
            d[p + 'w_in'] = normal((D_MODEL, POOL_IN), D_MODEL ** -0.5)
            d[p + 'pool_w'] = normal((N_POOL_GROUPS, POOL_GROUP, POOL_GROUP), POOL_GROUP ** -0.5)
            d[p + 'pool_scale'] = 1.0 + normal((MIXER_WIDTH,), 0.1)
        d[p + 'w_out'] = normal((D_MODEL, D_MODEL), BETA * D_MODEL ** -0.5)
        d[p + 'ln1_g'] = 1.0 + normal((D_MODEL,), 0.02)
        d[p + 'ln1_b'] = normal((D_MODEL,), 0.02)
        d[p + 'w_ff1'] = normal((D_MODEL, D_FF), D_MODEL ** -0.5)
        d[p + 'w_ff2'] = normal((D_FF, D_MODEL), BETA * D_FF ** -0.5)
        d[p + 'ln2_g'] = 1.0 + normal((D_MODEL,), 0.02)
        d[p + 'ln2_b'] = normal((D_MODEL,), 0.02)
    return d


def reference(x, mem, w_mem_kv,
              l0_w_in, l0_q_gain, l0_k_gain, l0_w_out, l0_ln1_g, l0_ln1_b, l0_w_ff1, l0_w_ff2, l0_ln2_g, l0_ln2_b,
              l1_w_in, l1_pool_w, l1_pool_scale, l1_w_out, l1_ln1_g, l1_ln1_b, l1_w_ff1, l1_w_ff2, l1_ln2_g, l1_ln2_b,
              l2_w_in, l2_q_gain, l2_k_gain, l2_w_out, l2_ln1_g, l2_ln1_b, l2_w_ff1, l2_w_ff2, l2_ln2_g, l2_ln2_b,
              l3_w_in, l3_pool_w, l3_pool_scale, l3_w_out, l3_ln1_g, l3_ln1_b, l3_w_ff1, l3_w_ff2, l3_ln2_g, l3_ln2_b):
    layers = [
        (l0_w_in, l0_q_gain, l0_k_gain, l0_w_out, l0_ln1_g, l0_ln1_b, l0_w_ff1, l0_w_ff2, l0_ln2_g, l0_ln2_b),
        (l1_w_in, l1_pool_w, l1_pool_scale, l1_w_out, l1_ln1_g, l1_ln1_b, l1_w_ff1, l1_w_ff2, l1_ln2_g, l1_ln2_b),
        (l2_w_in, l2_q_gain, l2_k_gain, l2_w_out, l2_ln1_g, l2_ln1_b, l2_w_ff1, l2_w_ff2, l2_ln2_g, l2_ln2_b),
        (l3_w_in, l3_pool_w, l3_pool_scale, l3_w_out, l3_ln1_g, l3_ln1_b, l3_w_ff1, l3_w_ff2, l3_ln2_g, l3_ln2_b),
    ]
    B, S, _ = x.shape
    mem_kv = mem @ w_mem_kv
    mem_k, mem_v = jnp.split(mem_kv, 2, axis=-1)
    mem_k = mem_k.reshape(B, mem.shape[1], MEM_HEADS, MEM_HEAD_DIM)
    mem_v = mem_v.reshape(B, mem.shape[1], MEM_HEADS, MEM_HEAD_DIM)
    cos, sin = axial_rope_tables(S)

    h = x
    for i in range(DEPTH):
        w_in, pa, pb, w_out, ln1_g, ln1_b, w_ff1, w_ff2, ln2_g, ln2_b = layers[i]
        if i % N_MIXERS == 0:
            y = attention_mix(h, w_in, pa, pb, w_out, mem_k, mem_v, cos, sin)
        else:
            y = pooling_mix(h, w_in, pa, pb, w_out, mem_k, mem_v)
        h = layer_norm(ALPHA * h + y, ln1_g, ln1_b)
        h = layer_norm(ALPHA * h + squared_relu_mlp(h, w_ff1, w_ff2), ln2_g, ln2_b)
    return h
```

```python
import functools
import math

import jax
import jax.numpy as jnp
from jax import lax
from jax.experimental import pallas as pl
from jax.experimental.pallas import tpu as pltpu

D_MODEL = 4096
DEPTH = 4
GRID_W = 64
N_MEM = 256
MEM_HEADS = 4
MEM_HEAD_DIM = 256
MEM_WIDTH = MEM_HEADS * MEM_HEAD_DIM
MIXER_WIDTH = D_MODEL - MEM_WIDTH
HEAD_DIM = 128
N_KV_HEADS = 6
Q_PER_KV = 4
GROUP_WIDTH = Q_PER_KV * HEAD_DIM
KV_WIDTH = N_KV_HEADS * HEAD_DIM
ROPE_THETA = 10000.0
N_FREQ = HEAD_DIM // 4
POOL_GROUP = MIXER_WIDTH // 4
POOL_HALO = 16
D_FF = 4 * D_MODEL
MLP_OUT_CHUNK = 512
LN_ROWS = 64
ALPHA = (2 * DEPTH) ** 0.25
LN_EPS = 1e-5
RMS_EPS = 1e-6

MIB = 1024 * 1024
BF16 = jnp.bfloat16
F32 = jnp.float32
NT_DIMS = (((1,), (1,)), ((), ()))


def _params(semantics, vmem_mib):
    return pltpu.CompilerParams(dimension_semantics=semantics, vmem_limit_bytes=vmem_mib * MIB)


def _mm_body(a_ref, b_ref, o_ref):
    o_ref[...] = jnp.dot(a_ref[...], b_ref[...], preferred_element_type=F32).astype(o_ref.dtype)


def _matmul(a, b, *, col0, ncols, tn, out_dtype, tm=1024):
    m, k = a.shape
    tm = min(tm, m)
    nb0 = col0 // tn
    assert col0 % tn == 0 and ncols % tn == 0 and m % tm == 0
    return pl.pallas_call(
        _mm_body,
        grid=(m // tm, ncols // tn),
        in_specs=[pl.BlockSpec((tm, k), lambda i, j: (i, 0)),
                  pl.BlockSpec((k, tn), lambda i, j: (0, j + nb0))],
        out_specs=pl.BlockSpec((tm, tn), lambda i, j: (i, j)),
        out_shape=jax.ShapeDtypeStruct((m, ncols), out_dtype),
        compiler_params=_params(("parallel", "arbitrary"), 48),
    )(a, b)


def _qk_body(a_ref, b_ref, gain_ref, cos_ref, sin_ref, o_ref):
    acc = jnp.dot(a_ref[...], b_ref[...], preferred_element_type=F32)
    tm, tn = acc.shape
    lane = lax.broadcasted_iota(jnp.int32, (tm, HEAD_DIM), 1)
    first_half = (lane & (N_FREQ)) == 0
    gain = gain_ref[...]
    cos = cos_ref[...]
    sin = sin_ref[...]
    for j in range(tn // HEAD_DIM):
        x = acc[:, j * HEAD_DIM:(j + 1) * HEAD_DIM]
        ms = jnp.mean(x * x, axis=-1, keepdims=True)
        xn = x * lax.rsqrt(ms + RMS_EPS) * gain
        partner = jnp.where(first_half,
                            pltpu.roll(xn, HEAD_DIM - N_FREQ, 1),
                            pltpu.roll(xn, N_FREQ, 1))
        o_ref[:, j * HEAD_DIM:(j + 1) * HEAD_DIM] = (xn * cos + partner * sin).astype(o_ref.dtype)


def _qk_proj(a, b, gain, cos_t, sin_t, *, col0, ncols, tn, tm=1024):
    m, k = a.shape
    tm = min(tm, m)
    nb0 = col0 // tn
    assert col0 % tn == 0 and ncols % tn == 0 and m % tm == 0
    return pl.pallas_call(
        _qk_body,
        grid=(m // tm, ncols // tn),
        in_specs=[pl.BlockSpec((tm, k), lambda i, j: (i, 0)),
                  pl.BlockSpec((k, tn), lambda i, j: (0, j + nb0)),
                  pl.BlockSpec((1, HEAD_DIM), lambda i, j: (0, 0)),
                  pl.BlockSpec((tm, HEAD_DIM), lambda i, j: (i, 0)),
                  pl.BlockSpec((tm, HEAD_DIM), lambda i, j: (i, 0))],
        out_specs=pl.BlockSpec((tm, tn), lambda i, j: (i, j)),
        out_shape=jax.ShapeDtypeStruct((m, ncols), BF16),
        compiler_params=_params(("parallel", "arbitrary"), 56),
    )(a, b, gain, cos_t, sin_t)


def _attn_body(q_ref, k_ref, v_ref, o_ref, *, tq, tk, nk):
    q4 = jnp.concatenate(
        [q_ref[:, g * HEAD_DIM:(g + 1) * HEAD_DIM] for g in range(Q_PER_KV)], axis=0)
    rows = Q_PER_KV * tq

    def step(c, carry):
        m, l, acc = carry
        off = pl.multiple_of(c * tk, tk)
        kc = k_ref[pl.ds(off, tk), :]
        vc = v_ref[pl.ds(off, tk), :]
        s = lax.dot_general(q4, kc, NT_DIMS, preferred_element_type=F32)
        m_new = jnp.maximum(m, jnp.max(s, axis=-1, keepdims=True))
        alpha = jnp.exp2(m - m_new)
        p = jnp.exp2(s - m_new)
        l = alpha * l + jnp.sum(p, axis=-1, keepdims=True)
        acc = alpha * acc + jnp.dot(p.astype(BF16), vc, preferred_element_type=F32)
        return m_new, l, acc

    m0 = jnp.full((rows, 1), -jnp.inf, F32)
    l0 = jnp.zeros((rows, 1), F32)
    acc0 = jnp.zeros((rows, HEAD_DIM), F32)
    _, l, acc = lax.fori_loop(0, nk, step, (m0, l0, acc0))
    out = acc * (1.0 / l)
    for g in range(Q_PER_KV):
        o_ref[:, g * HEAD_DIM:(g + 1) * HEAD_DIM] = out[g * tq:(g + 1) * tq].astype(o_ref.dtype)


def _attention(q, k, v, *, tq=256, tk=512):
    s_len = q.shape[0]
    tq = min(tq, s_len)
    tk = min(tk, s_len)
    body = functools.partial(_attn_body, tq=tq, tk=tk, nk=s_len // tk)
    return pl.pallas_call(
        body,
        grid=(N_KV_HEADS, s_len // tq),
        in_specs=[pl.BlockSpec((tq, GROUP_WIDTH), lambda h, i: (i, h)),
                  pl.BlockSpec((s_len, HEAD_DIM), lambda h, i: (0, h)),
                  pl.BlockSpec((s_len, HEAD_DIM), lambda h, i: (0, h))],
        out_specs=pl.BlockSpec((tq, GROUP_WIDTH), lambda h, i: (i, h)),
        out_shape=jax.ShapeDtypeStruct((s_len, D_MODEL), BF16),
        compiler_params=_params(("parallel", "arbitrary"), 48),
    )(q, k, v)


def _mem_body(qm_ref, mk_ref, mv_ref, y_hbm_ref, o_ref):
    del y_hbm_ref
    scale = MEM_HEAD_DIM ** -0.5
    for h in range(MEM_HEADS):
        cols = slice(h * MEM_HEAD_DIM, (h + 1) * MEM_HEAD_DIM)
        s = lax.dot_general(qm_ref[:, cols], mk_ref[:, cols], NT_DIMS,
                            preferred_element_type=F32) * scale
        p = jnp.exp(s - jnp.max(s, axis=-1, keepdims=True))
        p = p * (1.0 / jnp.sum(p, axis=-1, keepdims=True))
        o_ref[:, cols] = jnp.dot(p.astype(BF16), mv_ref[:, cols],
                                 preferred_element_type=F32).astype(o_ref.dtype)


def _mem_attention(qm_arr, qm_block, mem_kv, y, *, tm=1024):
    s_len = y.shape[0]
    tm = min(tm, s_len)
    out_block = MIXER_WIDTH // MEM_WIDTH
    return pl.pallas_call(
        _mem_body,
        grid=(s_len // tm,),
        in_specs=[pl.BlockSpec((tm, MEM_WIDTH), lambda i: (i, qm_block)),
                  pl.BlockSpec((N_MEM, MEM_WIDTH), lambda i: (0, 0)),
                  pl.BlockSpec((N_MEM, MEM_WIDTH), lambda i: (0, 1)),
                  pl.BlockSpec(memory_space=pl.ANY)],
        out_specs=pl.BlockSpec((tm, MEM_WIDTH), lambda i: (i, out_block)),
        out_shape=jax.ShapeDtypeStruct(y.shape, y.dtype),
        input_output_aliases={3: 0},
        compiler_params=_params(("parallel",), 32),
    )(qm_arr, mem_kv, mem_kv, y)


def _pool_body(up_ref, um_ref, un_ref, pw_ref, sc_ref, o_ref, *, tm, s_len):
    i = pl.program_id(0)
    g = pl.program_id(1)
    lo = jnp.left_shift(jnp.int32(1), g)
    hi = lo - 1
    u_ext = jnp.concatenate([up_ref[...], um_ref[...], un_ref[...]], axis=0)
    ext = tm + 2 * POOL_HALO
    t = lax.broadcasted_iota(jnp.int32, (tm, ext), 0)
    j = lax.broadcasted_iota(jnp.int32, (tm, ext), 1)
    d = j - POOL_HALO - t
    r = i * tm + j - POOL_HALO
    in_window = jnp.where(d >= -lo, 1.0, 0.0)
    in_window = jnp.where(d <= hi, in_window, 0.0)
    in_window = jnp.where(r >= 0, in_window, 0.0)
    in_window = jnp.where(r < s_len, in_window, 0.0)
    t_col = i * tm + lax.broadcasted_iota(jnp.int32, (tm, 1), 0)
    cnt = jnp.minimum(t_col + hi + 1, s_len) - jnp.maximum(t_col - lo, 0)
    cnt_f = cnt.astype(F32)
    band = jnp.where(d == 0, 1.0 - cnt_f, in_window).astype(BF16)
    pooled = jnp.dot(band, u_ext, preferred_element_type=F32) * (1.0 / cnt_f)
    y = jnp.dot(pooled.astype(BF16), pw_ref[0], preferred_element_type=F32) * sc_ref[...]
    o_ref[...] = y.astype(o_ref.dtype)


def _pool_mix(proj, pool_w, pool_scale, *, tm=512):
    s_len = proj.shape[0]
    tm = min(tm, s_len)
    hb = tm // POOL_HALO
    last = s_len // POOL_HALO - 1
    body = functools.partial(_pool_body, tm=tm, s_len=s_len)
    return pl.pallas_call(
        body,
        grid=(s_len // tm, 4),
        in_specs=[pl.BlockSpec((POOL_HALO, POOL_GROUP), lambda i, g: (jnp.maximum(i * hb - 1, 0), g)),
                  pl.BlockSpec((tm, POOL_GROUP), lambda i, g: (i, g)),
                  pl.BlockSpec((POOL_HALO, POOL_GROUP), lambda i, g: (jnp.minimum((i + 1) * hb, last), g)),
                  pl.BlockSpec((1, POOL_GROUP, POOL_GROUP), lambda i, g: (g, 0, 0)),
                  pl.BlockSpec((1, POOL_GROUP), lambda i, g: (0, g))],
        out_specs=pl.BlockSpec((tm, POOL_GROUP), lambda i, g: (i, g)),
        out_shape=jax.ShapeDtypeStruct((s_len, D_MODEL), BF16),
        compiler_params=_params(("parallel", "arbitrary"), 32),
    )(proj, proj, proj, pool_w, pool_scale)


def _deepnorm(h, y, gamma, beta):
    z = ALPHA * h + y
    mu = jnp.mean(z, axis=-1, keepdims=True)
    zc = z - mu
    var = jnp.mean(zc * zc, axis=-1, keepdims=True)
    return zc * lax.rsqrt(var + LN_EPS) * gamma + beta


def _ln_body(y_ref, h_ref, g_ref, b_ref, o_ref, ob_ref):
    out = _deepnorm(h_ref[...], y_ref[...], g_ref[...], b_ref[...])
    o_ref[...] = out
    ob_ref[...] = out.astype(ob_ref.dtype)


def _residual_ln(y, h, gamma, beta, *, tm=128):
    m, d = h.shape
    tm = min(tm, m)
    row = pl.BlockSpec((tm, d), lambda i: (i, 0))
    vec = pl.BlockSpec((1, d), lambda i: (0, 0))
    return pl.pallas_call(
        _ln_body,
        grid=(m // tm,),
        in_specs=[row, row, vec, vec],
        out_specs=[row, row],
        out_shape=[jax.ShapeDtypeStruct((m, d), F32), jax.ShapeDtypeStruct((m, d), BF16)],
        compiler_params=_params(("parallel",), 32),
    )(y, h, gamma, beta)


def _mlp_body(h_ref, hb_ref, w1_ref, w2_ref, g_ref, b_ref, o_ref, ob_ref, *, nf):
    f = pl.program_id(1)

    @pl.when(f == 0)
    def _():
        o_ref[...] = jnp.zeros_like(o_ref)

    a = jnp.dot(hb_ref[...], w1_ref[...], preferred_element_type=F32)
    a = jnp.square(jnp.maximum(a, 0.0)).astype(BF16)
    for n in range(0, o_ref.shape[1], MLP_OUT_CHUNK):
        cols = slice(n, n + MLP_OUT_CHUNK)
        o_ref[:, cols] += jnp.dot(a, w2_ref[:, cols], preferred_element_type=F32)

    @pl.when(f == nf - 1)
    def _():
        def rows_step(r, carry):
            rows = pl.ds(pl.multiple_of(r * LN_ROWS, LN_ROWS), LN_ROWS)
            out = _deepnorm(h_ref[rows, :], o_ref[rows, :], g_ref[...], b_ref[...])
            o_ref[rows, :] = out
            ob_ref[rows, :] = out.astype(ob_ref.dtype)
            return carry

        lax.fori_loop(0, o_ref.shape[0] // LN_ROWS, rows_step, 0)


def _mlp_ln(h, hb, w1, w2, gamma, beta, *, tm=512, tf=512):
    m, d = h.shape
    dff = w1.shape[1]
    tm = min(tm, m)
    body = functools.partial(_mlp_body, nf=dff // tf)
    row = lambda i, f: (i, 0)
    return pl.pallas_call(
        body,
        grid=(m // tm, dff // tf),
        in_specs=[pl.BlockSpec((tm, d), row, pipeline_mode=pl.Buffered(1)),
                  pl.BlockSpec((tm, d), row, pipeline_mode=pl.Buffered(1)),
                  pl.BlockSpec((d, tf), lambda i, f: (0, f)),
                  pl.BlockSpec((tf, d), lambda i, f: (f, 0)),
                  pl.BlockSpec((1, d), lambda i, f: (0, 0)),
                  pl.BlockSpec((1, d), lambda i, f: (0, 0))],
        out_specs=[pl.BlockSpec((tm, d), row), pl.BlockSpec((tm, d), row)],
        out_shape=[jax.ShapeDtypeStruct((m, d), F32), jax.ShapeDtypeStruct((m, d), BF16)],
        compiler_params=_params(("parallel", "arbitrary"), 56),
    )(h, hb, w1, w2, gamma, beta)


def _rope_tables(s_len):
    rows = s_len // GRID_W
    inv_freq = ROPE_THETA ** (-jnp.arange(N_FREQ, dtype=F32) / N_FREQ)
    row_ang = jnp.arange(rows, dtype=F32)[:, None, None] * inv_freq
    col_ang = jnp.arange(GRID_W, dtype=F32)[None, :, None] * inv_freq
    ang = jnp.stack([jnp.broadcast_to(row_ang, (rows, GRID_W, N_FREQ)),
                     jnp.broadcast_to(col_ang, (rows, GRID_W, N_FREQ))], axis=2).reshape(s_len, 2, 1, N_FREQ)
    cos_t = jnp.broadcast_to(jnp.cos(ang), (s_len, 2, 2, N_FREQ)).reshape(s_len, HEAD_DIM)
    sign = jnp.array([-1.0, 1.0], F32).reshape(1, 1, 2, 1)
    sin_t = (jnp.sin(ang) * sign).reshape(s_len, HEAD_DIM)
    return cos_t, sin_t


def _vec(p):
    return p.reshape(1, -1).astype(F32)


def kernel(x, mem, w_mem_kv, l0_w_in, l0_q_gain, l0_k_gain, l0_w_out, l0_ln1_g, l0_ln1_b, l0_w_ff1, l0_w_ff2, l0_ln2_g, l0_ln2_b, l1_w_in, l1_pool_w, l1_pool_scale, l1_w_out, l1_ln1_g, l1_ln1_b, l1_w_ff1, l1_w_ff2, l1_ln2_g, l1_ln2_b, l2_w_in, l2_q_gain, l2_k_gain, l2_w_out, l2_ln1_g, l2_ln1_b, l2_w_ff1, l2_w_ff2, l2_ln2_g, l2_ln2_b, l3_w_in, l3_pool_w, l3_pool_scale, l3_w_out, l3_ln1_g, l3_ln1_b, l3_w_ff1, l3_w_ff2, l3_ln2_g, l3_ln2_b):
    layers = [
        (l0_w_in, l0_q_gain, l0_k_gain, l0_w_out, l0_ln1_g, l0_ln1_b, l0_w_ff1, l0_w_ff2, l0_ln2_g, l0_ln2_b),
        (l1_w_in, l1_pool_w, l1_pool_scale, l1_w_out, l1_ln1_g, l1_ln1_b, l1_w_ff1, l1_w_ff2, l1_ln2_g, l1_ln2_b),
        (l2_w_in, l2_q_gain, l2_k_gain, l2_w_out, l2_ln1_g, l2_ln1_b, l2_w_ff1, l2_w_ff2, l2_ln2_g, l2_ln2_b),
        (l3_w_in, l3_pool_w, l3_pool_scale, l3_w_out, l3_ln1_g, l3_ln1_b, l3_w_ff1, l3_w_ff2, l3_ln2_g, l3_ln2_b),
    ]
    b, s_len, d = x.shape
    assert b == 1 and d == D_MODEL

    mem_kv = _matmul(mem[0].astype(BF16), w_mem_kv.astype(BF16),
                     col0=0, ncols=2 * MEM_WIDTH, tn=MEM_WIDTH, out_dtype=BF16)
    cos_t, sin_t = _rope_tables(s_len)
    q_scale = HEAD_DIM ** -0.5 * math.log2(math.e)
    cos_q, sin_q = cos_t * q_scale, sin_t * q_scale

    h = x[0]
    hb = h.astype(BF16)
    for i in range(DEPTH):
        w_in, pa, pb, w_out, ln1_g, ln1_b, w_ff1, w_ff2, ln2_g, ln2_b = layers[i]
        w_in = w_in.astype(BF16)
        if i % 2 == 0:
            q = _qk_proj(hb, w_in, _vec(pa), cos_q, sin_q, col0=0, ncols=MIXER_WIDTH, tn=1024)
            k = _qk_proj(hb, w_in, _vec(pb), cos_t, sin_t, col0=MIXER_WIDTH, ncols=KV_WIDTH, tn=KV_WIDTH)
            v = _matmul(hb, w_in, col0=MIXER_WIDTH + KV_WIDTH, ncols=KV_WIDTH, tn=KV_WIDTH, out_dtype=BF16)
            qm = _matmul(hb, w_in, col0=MIXER_WIDTH + 2 * KV_WIDTH, ncols=MEM_WIDTH, tn=512, out_dtype=BF16)
            y = _attention(q, k, v)
            y = _mem_attention(qm, 0, mem_kv, y)
        else:
            proj = _matmul(hb, w_in, col0=0, ncols=D_MODEL, tn=1024, out_dtype=BF16)
            y = _pool_mix(proj, pa.astype(BF16), _vec(pb))
            y = _mem_attention(proj, MIXER_WIDTH // MEM_WIDTH, mem_kv, y)
        y = _matmul(y, w_out.astype(BF16), col0=0, ncols=D_MODEL, tn=1024, out_dtype=F32)
        h, hb = _residual_ln(y, h, _vec(ln1_g), _vec(ln1_b))
        h, hb = _mlp_ln(h, hb, w_ff1.astype(BF16), w_ff2.astype(BF16), _vec(ln2_g), _vec(ln2_b))
    return h[None]
```

```python
import functools
import math

import jax
import jax.numpy as jnp
from jax import lax
from jax.experimental import pallas as pl
from jax.experimental.pallas import tpu as pltpu

D_MODEL = 4096
DEPTH = 4
GRID_W = 64
N_MEM = 256
MEM_HEADS = 4
MEM_HEAD_DIM = 256
MEM_WIDTH = MEM_HEADS * MEM_HEAD_DIM
MIXER_WIDTH = D_MODEL - MEM_WIDTH
HEAD_DIM = 128
N_KV_HEADS = 6
Q_PER_KV = 4
GROUP_WIDTH = Q_PER_KV * HEAD_DIM
KV_WIDTH = N_KV_HEADS * HEAD_DIM
ROPE_THETA = 10000.0
N_FREQ = HEAD_DIM // 4
POOL_GROUP = MIXER_WIDTH // 4
POOL_HALO = 16
D_FF = 4 * D_MODEL
MLP_OUT_CHUNK = 512
LN_ROWS = 64
ALPHA = (2 * DEPTH) ** 0.25
LN_EPS = 1e-5
RMS_EPS = 1e-6

MIB = 1024 * 1024
BF16 = jnp.bfloat16
F32 = jnp.float32
NT_DIMS = (((1,), (1,)), ((), ()))


def _params(semantics, vmem_mib):
    return pltpu.CompilerParams(dimension_semantics=semantics, vmem_limit_bytes=vmem_mib * MIB)


def _mm_body(a_ref, b_ref, o_ref, *, transpose_out):
    acc = jnp.dot(a_ref[...], b_ref[...], preferred_element_type=F32)
    if transpose_out:
        acc = acc.T
    o_ref[...] = acc.astype(o_ref.dtype)


def _matmul(a, b, *, col0, ncols, tn, out_dtype, tm=1024, transpose_out=False):
    m, k = a.shape
    tm = min(tm, m)
    nb0 = col0 // tn
    assert col0 % tn == 0 and ncols % tn == 0 and m % tm == 0
    if transpose_out:
        out_spec = pl.BlockSpec((tn, tm), lambda i, j: (j, i))
        out_shape = jax.ShapeDtypeStruct((ncols, m), out_dtype)
    else:
        out_spec = pl.BlockSpec((tm, tn), lambda i, j: (i, j))
        out_shape = jax.ShapeDtypeStruct((m, ncols), out_dtype)
    return pl.pallas_call(
        functools.partial(_mm_body, transpose_out=transpose_out),
        grid=(m // tm, ncols // tn),
        in_specs=[pl.BlockSpec((tm, k), lambda i, j: (i, 0)),
                  pl.BlockSpec((k, tn), lambda i, j: (0, j + nb0))],
        out_specs=out_spec,
        out_shape=out_shape,
        compiler_params=_params(("parallel", "arbitrary"), 48),
    )(a, b)


def _qk_body(a_ref, b_ref, gain_ref, cos_ref, sin_ref, o_ref, *, transpose_out):
    acc = jnp.dot(a_ref[...], b_ref[...], preferred_element_type=F32)
    tm, tn = acc.shape
    lane = lax.broadcasted_iota(jnp.int32, (tm, HEAD_DIM), 1)
    first_half = (lane & (N_FREQ)) == 0
    gain = gain_ref[...]
    cos = cos_ref[...]
    sin = sin_ref[...]
    for j in range(tn // HEAD_DIM):
        x = acc[:, j * HEAD_DIM:(j + 1) * HEAD_DIM]
        ms = jnp.mean(x * x, axis=-1, keepdims=True)
        xn = x * lax.rsqrt(ms + RMS_EPS) * gain
        partner = jnp.where(first_half,
                            pltpu.roll(xn, HEAD_DIM - N_FREQ, 1),
                            pltpu.roll(xn, N_FREQ, 1))
        out = xn * cos + partner * sin
        if transpose_out:
            o_ref[j * HEAD_DIM:(j + 1) * HEAD_DIM, :] = out.T.astype(o_ref.dtype)
        else:
            o_ref[:, j * HEAD_DIM:(j + 1) * HEAD_DIM] = out.astype(o_ref.dtype)


def _qk_proj(a, b, gain, cos_t, sin_t, *, col0, ncols, tn, tm=1024, transpose_out=False):
    m, k = a.shape
    tm = min(tm, m)
    nb0 = col0 // tn
    assert col0 % tn == 0 and ncols % tn == 0 and m % tm == 0
    if transpose_out:
        out_spec = pl.BlockSpec((tn, tm), lambda i, j: (j, i))
        out_shape = jax.ShapeDtypeStruct((ncols, m), BF16)
    else:
        out_spec = pl.BlockSpec((tm, tn), lambda i, j: (i, j))
        out_shape = jax.ShapeDtypeStruct((m, ncols), BF16)
    return pl.pallas_call(
        functools.partial(_qk_body, transpose_out=transpose_out),
        grid=(m // tm, ncols // tn),
        in_specs=[pl.BlockSpec((tm, k), lambda i, j: (i, 0)),
                  pl.BlockSpec((k, tn), lambda i, j: (0, j + nb0)),
                  pl.BlockSpec((1, HEAD_DIM), lambda i, j: (0, 0)),
                  pl.BlockSpec((tm, HEAD_DIM), lambda i, j: (i, 0)),
                  pl.BlockSpec((tm, HEAD_DIM), lambda i, j: (i, 0))],
        out_specs=out_spec,
        out_shape=out_shape,
        compiler_params=_params(("parallel", "arbitrary"), 56),
    )(a, b, gain, cos_t, sin_t)


def _attn_body(qt_ref, k_ref, vt_ref, o_ref, st_a, st_b, m_ref, l_ref, acc_ref, *, tk, nk):
    def scores(kc, g, dst):
        dst[g] = jnp.dot(kc, qt_ref[g * HEAD_DIM:(g + 1) * HEAD_DIM, :],
                         preferred_element_type=F32)

    def key_chunk(c):
        return k_ref[pl.ds(pl.multiple_of(c * tk, tk), tk), :]

    def chunk(c, src, dst):
        kc_next = key_chunk(jnp.minimum(c + 1, nk - 1))
        vtc = vt_ref[:, pl.ds(pl.multiple_of(c * tk, tk), tk)]
        for g in range(Q_PER_KV):
            scores(kc_next, g, dst)
            st = src[g]
            m = m_ref[g]
            m_new = jnp.maximum(m, jnp.max(st, axis=0, keepdims=True))
            alpha = jnp.exp2(m - m_new)
            pt = jnp.exp2(st - m_new)
            l_ref[g] = alpha * l_ref[g] + jnp.sum(pt, axis=0, keepdims=True)
            m_ref[g] = m_new
            acc_ref[g] = alpha * acc_ref[g] + jnp.dot(vtc, pt.astype(BF16),
                                                      preferred_element_type=F32)

    m_ref[...] = jnp.full(m_ref.shape, -jnp.inf, F32)
    l_ref[...] = jnp.zeros(l_ref.shape, F32)
    acc_ref[...] = jnp.zeros(acc_ref.shape, F32)
    for g in range(Q_PER_KV):
        scores(key_chunk(0), g, st_a)

    def pair(j, carry):
        chunk(2 * j, st_a, st_b)
        chunk(2 * j + 1, st_b, st_a)
        return carry

    lax.fori_loop(0, nk // 2, pair, 0)
    for g in range(Q_PER_KV):
        out_t = acc_ref[g] * (1.0 / l_ref[g])
        o_ref[:, g * HEAD_DIM:(g + 1) * HEAD_DIM] = out_t.T.astype(o_ref.dtype)


def _attention(qt, k, vt, *, tq=512, tk=512):
    s_len = k.shape[0]
    tq = min(tq, s_len)
    tk = min(tk, s_len)
    assert s_len % (2 * tk) == 0 and s_len % tq == 0
    body = functools.partial(_attn_body, tk=tk, nk=s_len // tk)
    return pl.pallas_call(
        body,
        grid=(N_KV_HEADS, s_len // tq),
        in_specs=[pl.BlockSpec((GROUP_WIDTH, tq), lambda h, i: (h, i)),
                  pl.BlockSpec((s_len, HEAD_DIM), lambda h, i: (0, h)),
                  pl.BlockSpec((HEAD_DIM, s_len), lambda h, i: (h, 0))],
        out_specs=pl.BlockSpec((tq, GROUP_WIDTH), lambda h, i: (i, h)),
        out_shape=jax.ShapeDtypeStruct((s_len, D_MODEL), BF16),
        scratch_shapes=[pltpu.VMEM((Q_PER_KV, tk, tq), F32),
                        pltpu.VMEM((Q_PER_KV, tk, tq), F32),
                        pltpu.VMEM((Q_PER_KV, 1, tq), F32),
                        pltpu.VMEM((Q_PER_KV, 1, tq), F32),
                        pltpu.VMEM((Q_PER_KV, HEAD_DIM, tq), F32)],
        compiler_params=_params(("parallel", "arbitrary"), 48),
    )(qt, k, vt)


def _mem_body(qm_ref, mk_ref, mv_ref, y_hbm_ref, o_ref):
    del y_hbm_ref
    scale = MEM_HEAD_DIM ** -0.5
    for h in range(MEM_HEADS):
        cols = slice(h * MEM_HEAD_DIM, (h + 1) * MEM_HEAD_DIM)
        s = lax.dot_general(qm_ref[:, cols], mk_ref[:, cols], NT_DIMS,
                            preferred_element_type=F32) * scale
        p = jnp.exp(s - jnp.max(s, axis=-1, keepdims=True))
        p = p * (1.0 / jnp.sum(p, axis=-1, keepdims=True))
        o_ref[:, cols] = jnp.dot(p.astype(BF16), mv_ref[:, cols],
                                 preferred_element_type=F32).astype(o_ref.dtype)


def _mem_attention(qm_arr, qm_block, mem_kv, y, *, tm=1024):
    s_len = y.shape[0]
    tm = min(tm, s_len)
    out_block = MIXER_WIDTH // MEM_WIDTH
    return pl.pallas_call(
        _mem_body,
        grid=(s_len // tm,),
        in_specs=[pl.BlockSpec((tm, MEM_WIDTH), lambda i: (i, qm_block)),
                  pl.BlockSpec((N_MEM, MEM_WIDTH), lambda i: (0, 0)),
                  pl.BlockSpec((N_MEM, MEM_WIDTH), lambda i: (0, 1)),
                  pl.BlockSpec(memory_space=pl.ANY)],
        out_specs=pl.BlockSpec((tm, MEM_WIDTH), lambda i: (i, out_block)),
        out_shape=jax.ShapeDtypeStruct(y.shape, y.dtype),
        input_output_aliases={3: 0},
        compiler_params=_params(("parallel",), 32),
    )(qm_arr, mem_kv, mem_kv, y)


def _pool_body(up_ref, um_ref, un_ref, pw_ref, sc_ref, o_ref, *, tm, s_len):
    i = pl.program_id(0)
    g = pl.program_id(1)
    lo = jnp.left_shift(jnp.int32(1), g)
    hi = lo - 1
    u_ext = jnp.concatenate([up_ref[...], um_ref[...], un_ref[...]], axis=0)
    ext = tm + 2 * POOL_HALO
    t = lax.broadcasted_iota(jnp.int32, (tm, ext), 0)
    j = lax.broadcasted_iota(jnp.int32, (tm, ext), 1)
    d = j - POOL_HALO - t
    r = i * tm + j - POOL_HALO
    in_window = jnp.where(d >= -lo, 1.0, 0.0)
    in_window = jnp.where(d <= hi, in_window, 0.0)
    in_window = jnp.where(r >= 0, in_window, 0.0)
    in_window = jnp.where(r < s_len, in_window, 0.0)
    t_col = i * tm + lax.broadcasted_iota(jnp.int32, (tm, 1), 0)
    cnt = jnp.minimum(t_col + hi + 1, s_len) - jnp.maximum(t_col - lo, 0)
    cnt_f = cnt.astype(F32)
    band = jnp.where(d == 0, 1.0 - cnt_f, in_window).astype(BF16)
    pooled = jnp.dot(band, u_ext, preferred_element_type=F32) * (1.0 / cnt_f)
    y = jnp.dot(pooled.astype(BF16), pw_ref[0], preferred_element_type=F32) * sc_ref[...]
    o_ref[...] = y.astype(o_ref.dtype)


def _pool_mix(proj, pool_w, pool_scale, *, tm=512):
    s_len = proj.shape[0]
    tm = min(tm, s_len)
    hb = tm // POOL_HALO
    last = s_len // POOL_HALO - 1
    body = functools.partial(_pool_body, tm=tm, s_len=s_len)
    return pl.pallas_call(
        body,
        grid=(s_len // tm, 4),
        in_specs=[pl.BlockSpec((POOL_HALO, POOL_GROUP), lambda i, g: (jnp.maximum(i * hb - 1, 0), g)),
                  pl.BlockSpec((tm, POOL_GROUP), lambda i, g: (i, g)),
                  pl.BlockSpec((POOL_HALO, POOL_GROUP), lambda i, g: (jnp.minimum((i + 1) * hb, last), g)),
                  pl.BlockSpec((1, POOL_GROUP, POOL_GROUP), lambda i, g: (g, 0, 0)),
                  pl.BlockSpec((1, POOL_GROUP), lambda i, g: (0, g))],
        out_specs=pl.BlockSpec((tm, POOL_GROUP), lambda i, g: (i, g)),
        out_shape=jax.ShapeDtypeStruct((s_len, D_MODEL), BF16),
        compiler_params=_params(("parallel", "arbitrary"), 32),
    )(proj, proj, proj, pool_w, pool_scale)


def _deepnorm(h, y, gamma, beta):
    z = ALPHA * h + y
    mu = jnp.mean(z, axis=-1, keepdims=True)
    zc = z - mu
    var = jnp.mean(zc * zc, axis=-1, keepdims=True)
    return zc * lax.rsqrt(var + LN_EPS) * gamma + beta


def _ln_body(y_ref, h_ref, g_ref, b_ref, o_ref, ob_ref):
    out = _deepnorm(h_ref[...], y_ref[...], g_ref[...], b_ref[...])
    o_ref[...] = out
    ob_ref[...] = out.astype(ob_ref.dtype)


def _residual_ln(y, h, gamma, beta, *, tm=128):
    m, d = h.shape
    tm = min(tm, m)
    row = pl.BlockSpec((tm, d), lambda i: (i, 0))
    vec = pl.BlockSpec((1, d), lambda i: (0, 0))
    return pl.pallas_call(
        _ln_body,
        grid=(m // tm,),
        in_specs=[row, row, vec, vec],
        out_specs=[row, row],
        out_shape=[jax.ShapeDtypeStruct((m, d), F32), jax.ShapeDtypeStruct((m, d), BF16)],
        compiler_params=_params(("parallel",), 32),
    )(y, h, gamma, beta)


def _mlp_body(h_ref, hb_ref, w1_ref, w2_ref, g_ref, b_ref, o_ref, ob_ref, *, nf):
    f = pl.program_id(1)

    @pl.when(f == 0)
    def _():
        o_ref[...] = jnp.zeros_like(o_ref)

    a = jnp.dot(hb_ref[...], w1_ref[...], preferred_element_type=F32)
    a = jnp.square(jnp.maximum(a, 0.0)).astype(BF16)
    for n in range(0, o_ref.shape[1], MLP_OUT_CHUNK):
        cols = slice(n, n + MLP_OUT_CHUNK)
        o_ref[:, cols] += jnp.dot(a, w2_ref[:, cols], preferred_element_type=F32)

    @pl.when(f == nf - 1)
    def _():
        def rows_step(r, carry):
            rows = pl.ds(pl.multiple_of(r * LN_ROWS, LN_ROWS), LN_ROWS)
            out = _deepnorm(h_ref[rows, :], o_ref[rows, :], g_ref[...], b_ref[...])
            o_ref[rows, :] = out
            ob_ref[rows, :] = out.astype(ob_ref.dtype)
            return carry

        lax.fori_loop(0, o_ref.shape[0] // LN_ROWS, rows_step, 0)


def _mlp_ln(h, hb, w1, w2, gamma, beta, *, tm=512, tf=512):
    m, d = h.shape
    dff = w1.shape[1]
    tm = min(tm, m)
    body = functools.partial(_mlp_body, nf=dff // tf)
    row = lambda i, f: (i, 0)
    return pl.pallas_call(
        body,
        grid=(m // tm, dff // tf),
        in_specs=[pl.BlockSpec((tm, d), row, pipeline_mode=pl.Buffered(1)),
                  pl.BlockSpec((tm, d), row, pipeline_mode=pl.Buffered(1)),
                  pl.BlockSpec((d, tf), lambda i, f: (0, f)),
                  pl.BlockSpec((tf, d), lambda i, f: (f, 0)),
                  pl.BlockSpec((1, d), lambda i, f: (0, 0)),
                  pl.BlockSpec((1, d), lambda i, f: (0, 0))],
        out_specs=[pl.BlockSpec((tm, d), row), pl.BlockSpec((tm, d), row)],
        out_shape=[jax.ShapeDtypeStruct((m, d), F32), jax.ShapeDtypeStruct((m, d), BF16)],
        compiler_params=_params(("parallel", "arbitrary"), 56),
    )(h, hb, w1, w2, gamma, beta)


def _rope_tables(s_len):
    rows = s_len // GRID_W
    inv_freq = ROPE_THETA ** (-jnp.arange(N_FREQ, dtype=F32) / N_FREQ)
    row_ang = jnp.arange(rows, dtype=F32)[:, None, None] * inv_freq
    col_ang = jnp.arange(GRID_W, dtype=F32)[None, :, None] * inv_freq
    ang = jnp.stack([jnp.broadcast_to(row_ang, (rows, GRID_W, N_FREQ)),
                     jnp.broadcast_to(col_ang, (rows, GRID_W, N_FREQ))], axis=2).reshape(s_len, 2, 1, N_FREQ)
    cos_t = jnp.broadcast_to(jnp.cos(ang), (s_len, 2, 2, N_FREQ)).reshape(s_len, HEAD_DIM)
    sign = jnp.array([-1.0, 1.0], F32).reshape(1, 1, 2, 1)
    sin_t = (jnp.sin(ang) * sign).reshape(s_len, HEAD_DIM)
    return cos_t, sin_t


def _vec(p):
    return p.reshape(1, -1).astype(F32)


def kernel(x, mem, w_mem_kv, l0_w_in, l0_q_gain, l0_k_gain, l0_w_out, l0_ln1_g, l0_ln1_b, l0_w_ff1, l0_w_ff2, l0_ln2_g, l0_ln2_b, l1_w_in, l1_pool_w, l1_pool_scale, l1_w_out, l1_ln1_g, l1_ln1_b, l1_w_ff1, l1_w_ff2, l1_ln2_g, l1_ln2_b, l2_w_in, l2_q_gain, l2_k_gain, l2_w_out, l2_ln1_g, l2_ln1_b, l2_w_ff1, l2_w_ff2, l2_ln2_g, l2_ln2_b, l3_w_in, l3_pool_w, l3_pool_scale, l3_w_out, l3_ln1_g, l3_ln1_b, l3_w_ff1, l3_w_ff2, l3_ln2_g, l3_ln2_b):
    layers = [
        (l0_w_in, l0_q_gain, l0_k_gain, l0_w_out, l0_ln1_g, l0_ln1_b, l0_w_ff1, l0_w_ff2, l0_ln2_g, l0_ln2_b),
        (l1_w_in, l1_pool_w, l1_pool_scale, l1_w_out, l1_ln1_g, l1_ln1_b, l1_w_ff1, l1_w_ff2, l1_ln2_g, l1_ln2_b),
        (l2_w_in, l2_q_gain, l2_k_gain, l2_w_out, l2_ln1_g, l2_ln1_b, l2_w_ff1, l2_w_ff2, l2_ln2_g, l2_ln2_b),
        (l3_w_in, l3_pool_w, l3_pool_scale, l3_w_out, l3_ln1_g, l3_ln1_b, l3_w_ff1, l3_w_ff2, l3_ln2_g, l3_ln2_b),
    ]
    b, s_len, d = x.shape
    assert b == 1 and d == D_MODEL

    mem_kv = _matmul(mem[0].astype(BF16), w_mem_kv.astype(BF16),
                     col0=0, ncols=2 * MEM_WIDTH, tn=MEM_WIDTH, out_dtype=BF16)
    cos_t, sin_t = _rope_tables(s_len)
    q_scale = HEAD_DIM ** -0.5 * math.log2(math.e)
    cos_q, sin_q = cos_t * q_scale, sin_t * q_scale

    h = x[0]
    hb = h.astype(BF16)
    for i in range(DEPTH):
        w_in, pa, pb, w_out, ln1_g, ln1_b, w_ff1, w_ff2, ln2_g, ln2_b = layers[i]
        w_in = w_in.astype(BF16)
        if i % 2 == 0:
            qt = _qk_proj(hb, w_in, _vec(pa), cos_q, sin_q, col0=0, ncols=MIXER_WIDTH, tn=1024,
                          transpose_out=True)
            k = _qk_proj(hb, w_in, _vec(pb), cos_t, sin_t, col0=MIXER_WIDTH, ncols=KV_WIDTH, tn=KV_WIDTH)
            vt = _matmul(hb, w_in, col0=MIXER_WIDTH + KV_WIDTH, ncols=KV_WIDTH, tn=KV_WIDTH, out_dtype=BF16,
                         transpose_out=True)
            qm = _matmul(hb, w_in, col0=MIXER_WIDTH + 2 * KV_WIDTH, ncols=MEM_WIDTH, tn=512, out_dtype=BF16)
            y = _attention(qt, k, vt)
            y = _mem_attention(qm, 0, mem_kv, y)
        else:
            proj = _matmul(hb, w_in, col0=0, ncols=D_MODEL, tn=1024, out_dtype=BF16)
            y = _pool_mix(proj, pa.astype(BF16), _vec(pb))
            y = _mem_attention(proj, MIXER_WIDTH // MEM_WIDTH, mem_kv, y)
        y = _matmul(y, w_out.astype(BF16), col0=0, ncols=D_MODEL, tn=1024, out_dtype=F32)
        h, hb = _residual_ln(y, h, _vec(ln1_g), _vec(ln1_b))
        h, hb = _mlp_ln(h, hb, w_ff1.astype(BF16), w_ff2.astype(BF16), _vec(ln2_g), _vec(ln2_b))
    return h[None]
```

```python
import functools
import math

import jax
import jax.numpy as jnp
from jax import lax
from jax.experimental import pallas as pl
from jax.experimental.pallas import tpu as pltpu

D_MODEL = 4096
DEPTH = 4
GRID_W = 64
N_MEM = 256
MEM_HEADS = 4
MEM_HEAD_DIM = 256
MEM_WIDTH = MEM_HEADS * MEM_HEAD_DIM
MIXER_WIDTH = D_MODEL - MEM_WIDTH
HEAD_DIM = 128
N_KV_HEADS = 6
Q_PER_KV = 4
GROUP_WIDTH = Q_PER_KV * HEAD_DIM
KV_WIDTH = N_KV_HEADS * HEAD_DIM
DENOM_ROWS = 16
ATTN_KEY_CHUNK = 512
ROPE_THETA = 10000.0
N_FREQ = HEAD_DIM // 4
POOL_GROUP = MIXER_WIDTH // 4
POOL_HALO = 16
D_FF = 4 * D_MODEL
MLP_OUT_CHUNK = 512
LN_ROWS = 64
ALPHA = (2 * DEPTH) ** 0.25
LN_EPS = 1e-5
RMS_EPS = 1e-6

MIB = 1024 * 1024
BF16 = jnp.bfloat16
F32 = jnp.float32
NT_DIMS = (((1,), (1,)), ((), ()))


def _params(semantics, vmem_mib):
    return pltpu.CompilerParams(dimension_semantics=semantics, vmem_limit_bytes=vmem_mib * MIB)


def _mm_body(a_ref, b_ref, o_ref):
    o_ref[...] = jnp.dot(a_ref[...], b_ref[...], preferred_element_type=F32).astype(o_ref.dtype)


def _matmul(a, b, *, col0, ncols, tn, out_dtype, tm=1024):
    m, k = a.shape
    tm = min(tm, m)
    nb0 = col0 // tn
    assert col0 % tn == 0 and ncols % tn == 0 and m % tm == 0
    return pl.pallas_call(
        _mm_body,
        grid=(m // tm, ncols // tn),
        in_specs=[pl.BlockSpec((tm, k), lambda i, j: (i, 0)),
                  pl.BlockSpec((k, tn), lambda i, j: (0, j + nb0))],
        out_specs=pl.BlockSpec((tm, tn), lambda i, j: (i, j)),
        out_shape=jax.ShapeDtypeStruct((m, ncols), out_dtype),
        compiler_params=_params(("parallel", "arbitrary"), 48),
    )(a, b)


def _v_body(a_ref, b_ref, o_ref, *, tk):
    acc_t = jnp.dot(a_ref[...], b_ref[...], preferred_element_type=F32).T
    ones = jnp.ones((DENOM_ROWS, tk), o_ref.dtype)
    for h in range(N_KV_HEADS):
        for c in range(o_ref.shape[1]):
            o_ref[h, c, :HEAD_DIM, :] = acc_t[h * HEAD_DIM:(h + 1) * HEAD_DIM,
                                              c * tk:(c + 1) * tk].astype(o_ref.dtype)
            o_ref[h, c, HEAD_DIM:, :] = ones


def _v_proj(a, b, *, col0, tk, tm=1024):
    m, k = a.shape
    tm = min(tm, m)
    assert col0 % KV_WIDTH == 0 and m % tm == 0 and tm % tk == 0
    nb0 = col0 // KV_WIDTH
    return pl.pallas_call(
        functools.partial(_v_body, tk=tk),
        grid=(m // tm,),
        in_specs=[pl.BlockSpec((tm, k), lambda i: (i, 0)),
                  pl.BlockSpec((k, KV_WIDTH), lambda i: (0, nb0))],
        out_specs=pl.BlockSpec((N_KV_HEADS, tm // tk, HEAD_DIM + DENOM_ROWS, tk), lambda i: (0, i, 0, 0)),
        out_shape=jax.ShapeDtypeStruct((N_KV_HEADS, m // tk, HEAD_DIM + DENOM_ROWS, tk), BF16),
        compiler_params=_params(("parallel",), 48),
    )(a, b)


def _qk_body(a_ref, b_ref, gain_ref, cos_ref, sin_ref, o_ref, *, transpose_out):
    acc = jnp.dot(a_ref[...], b_ref[...], preferred_element_type=F32)
    tm, tn = acc.shape
    lane = lax.broadcasted_iota(jnp.int32, (tm, HEAD_DIM), 1)
    first_half = (lane & (N_FREQ)) == 0
    gain = gain_ref[...]
    cos = cos_ref[...]
    sin = sin_ref[...]
    for j in range(tn // HEAD_DIM):
        x = acc[:, j * HEAD_DIM:(j + 1) * HEAD_DIM]
        ms = jnp.mean(x * x, axis=-1, keepdims=True)
        xn = x * lax.rsqrt(ms + RMS_EPS) * gain
        partner = jnp.where(first_half,
                            pltpu.roll(xn, HEAD_DIM - N_FREQ, 1),
                            pltpu.roll(xn, N_FREQ, 1))
        out = xn * cos + partner * sin
        if transpose_out:
            o_ref[j * HEAD_DIM:(j + 1) * HEAD_DIM, :] = out.T.astype(o_ref.dtype)
        else:
            o_ref[:, j * HEAD_DIM:(j + 1) * HEAD_DIM] = out.astype(o_ref.dtype)


def _qk_proj(a, b, gain, cos_t, sin_t, *, col0, ncols, tn, tm=1024, transpose_out=False):
    m, k = a.shape
    tm = min(tm, m)
    nb0 = col0 // tn
    assert col0 % tn == 0 and ncols % tn == 0 and m % tm == 0
    if transpose_out:
        out_spec = pl.BlockSpec((tn, tm), lambda i, j: (j, i))
        out_shape = jax.ShapeDtypeStruct((ncols, m), BF16)
    else:
        out_spec = pl.BlockSpec((tm, tn), lambda i, j: (i, j))
        out_shape = jax.ShapeDtypeStruct((m, ncols), BF16)
    return pl.pallas_call(
        functools.partial(_qk_body, transpose_out=transpose_out),
        grid=(m // tm, ncols // tn),
        in_specs=[pl.BlockSpec((tm, k), lambda i, j: (i, 0)),
                  pl.BlockSpec((k, tn), lambda i, j: (0, j + nb0)),
                  pl.BlockSpec((1, HEAD_DIM), lambda i, j: (0, 0)),
                  pl.BlockSpec((tm, HEAD_DIM), lambda i, j: (i, 0)),
                  pl.BlockSpec((tm, HEAD_DIM), lambda i, j: (i, 0))],
        out_specs=out_spec,
        out_shape=out_shape,
        compiler_params=_params(("parallel", "arbitrary"), 56),
    )(a, b, gain, cos_t, sin_t)


def _attn_body(qt_ref, k_ref, vt_ref, o_ref, st_a, st_b, mx_a, mx_b, m_ref, acc_ref, *, tk, nk):
    def scores(kc, g, dst, mx_dst):
        st = jnp.dot(kc, qt_ref[g * HEAD_DIM:(g + 1) * HEAD_DIM, :],
                     preferred_element_type=F32)
        dst[g] = st
        mx_dst[g] = jnp.max(st, axis=0, keepdims=True)

    def key_chunk(c):
        return k_ref[pl.ds(pl.multiple_of(c * tk, tk), tk), :]

    def chunk(c, src, mx_src, dst, mx_dst):
        kc_next = key_chunk(jnp.minimum(c + 1, nk - 1))
        vtc = vt_ref[0, c]
        for g in range(Q_PER_KV):
            scores(kc_next, g, dst, mx_dst)
            m = m_ref[g]
            m_new = jnp.maximum(m, mx_src[g])
            alpha = jnp.exp2(m - m_new)
            pt = jnp.exp2(src[g] - m_new).astype(BF16)
            m_ref[g] = m_new
            acc_ref[g] = alpha * acc_ref[g] + jnp.dot(vtc, pt,
                                                      preferred_element_type=F32)

    m_ref[...] = jnp.full(m_ref.shape, -jnp.inf, F32)
    acc_ref[...] = jnp.zeros(acc_ref.shape, F32)
    for g in range(Q_PER_KV):
        scores(key_chunk(0), g, st_a, mx_a)

    def pair(j, carry):
        chunk(2 * j, st_a, mx_a, st_b, mx_b)
        chunk(2 * j + 1, st_b, mx_b, st_a, mx_a)
        return carry

    lax.fori_loop(0, nk // 2, pair, 0)
    for g in range(Q_PER_KV):
        denom = acc_ref[g, HEAD_DIM:HEAD_DIM + 1, :]
        out_t = acc_ref[g, :HEAD_DIM, :] * (1.0 / denom)
        o_ref[:, g * HEAD_DIM:(g + 1) * HEAD_DIM] = out_t.T.astype(o_ref.dtype)


def _attention(qt, k, vt, *, tq=512):
    s_len = k.shape[0]
    nk, tk = vt.shape[1], vt.shape[3]
    tq = min(tq, s_len)
    assert nk * tk == s_len and nk % 2 == 0 and s_len % tq == 0
    body = functools.partial(_attn_body, tk=tk, nk=nk)
    return pl.pallas_call(
        body,
        grid=(N_KV_HEADS, s_len // tq),
        in_specs=[pl.BlockSpec((GROUP_WIDTH, tq), lambda h, i: (h, i)),
                  pl.BlockSpec((s_len, HEAD_DIM), lambda h, i: (0, h)),
                  pl.BlockSpec((1, nk, HEAD_DIM + DENOM_ROWS, tk), lambda h, i: (h, 0, 0, 0))],
        out_specs=pl.BlockSpec((tq, GROUP_WIDTH), lambda h, i: (i, h)),
        out_shape=jax.ShapeDtypeStruct((s_len, D_MODEL), BF16),
        scratch_shapes=[pltpu.VMEM((Q_PER_KV, tk, tq), F32),
                        pltpu.VMEM((Q_PER_KV, tk, tq), F32),
                        pltpu.VMEM((Q_PER_KV, 1, tq), F32),
                        pltpu.VMEM((Q_PER_KV, 1, tq), F32),
                        pltpu.VMEM((Q_PER_KV, 1, tq), F32),
                        pltpu.VMEM((Q_PER_KV, HEAD_DIM + DENOM_ROWS, tq), F32)],
        compiler_params=_params(("parallel", "arbitrary"), 48),
    )(qt, k, vt)


def _mem_body(qm_ref, mk_ref, mv_ref, y_hbm_ref, o_ref):
    del y_hbm_ref
    scale = MEM_HEAD_DIM ** -0.5
    for h in range(MEM_HEADS):
        cols = slice(h * MEM_HEAD_DIM, (h + 1) * MEM_HEAD_DIM)
        s = lax.dot_general(qm_ref[:, cols], mk_ref[:, cols], NT_DIMS,
                            preferred_element_type=F32) * scale
        p = jnp.exp(s - jnp.max(s, axis=-1, keepdims=True))
        p = p * (1.0 / jnp.sum(p, axis=-1, keepdims=True))
        o_ref[:, cols] = jnp.dot(p.astype(BF16), mv_ref[:, cols],
                                 preferred_element_type=F32).astype(o_ref.dtype)


def _mem_attention(qm_arr, qm_block, mem_kv, y, *, tm=1024):
    s_len = y.shape[0]
    tm = min(tm, s_len)
    out_block = MIXER_WIDTH // MEM_WIDTH
    return pl.pallas_call(
        _mem_body,
        grid=(s_len // tm,),
        in_specs=[pl.BlockSpec((tm, MEM_WIDTH), lambda i: (i, qm_block)),
                  pl.BlockSpec((N_MEM, MEM_WIDTH), lambda i: (0, 0)),
                  pl.BlockSpec((N_MEM, MEM_WIDTH), lambda i: (0, 1)),
                  pl.BlockSpec(memory_space=pl.ANY)],
        out_specs=pl.BlockSpec((tm, MEM_WIDTH), lambda i: (i, out_block)),
        out_shape=jax.ShapeDtypeStruct(y.shape, y.dtype),
        input_output_aliases={3: 0},
        compiler_params=_params(("parallel",), 32),
    )(qm_arr, mem_kv, mem_kv, y)


def _pool_body(up_ref, um_ref, un_ref, pw_ref, sc_ref, o_ref, *, tm, s_len):
    i = pl.program_id(0)
    g = pl.program_id(1)
    lo = jnp.left_shift(jnp.int32(1), g)
    hi = lo - 1
    u_ext = jnp.concatenate([up_ref[...], um_ref[...], un_ref[...]], axis=0)
    ext = tm + 2 * POOL_HALO
    t = lax.broadcasted_iota(jnp.int32, (tm, ext), 0)
    j = lax.broadcasted_iota(jnp.int32, (tm, ext), 1)
    d = j - POOL_HALO - t
    r = i * tm + j - POOL_HALO
    in_window = jnp.where(d >= -lo, 1.0, 0.0)
    in_window = jnp.where(d <= hi, in_window, 0.0)
    in_window = jnp.where(r >= 0, in_window, 0.0)
    in_window = jnp.where(r < s_len, in_window, 0.0)
    t_col = i * tm + lax.broadcasted_iota(jnp.int32, (tm, 1), 0)
    cnt = jnp.minimum(t_col + hi + 1, s_len) - jnp.maximum(t_col - lo, 0)
    cnt_f = cnt.astype(F32)
    band = jnp.where(d == 0, 1.0 - cnt_f, in_window).astype(BF16)
    pooled = jnp.dot(band, u_ext, preferred_element_type=F32) * (1.0 / cnt_f)
    y = jnp.dot(pooled.astype(BF16), pw_ref[0], preferred_element_type=F32) * sc_ref[...]
    o_ref[...] = y.astype(o_ref.dtype)


def _pool_mix(proj, pool_w, pool_scale, *, tm=512):
    s_len = proj.shape[0]
    tm = min(tm, s_len)
    hb = tm // POOL_HALO
    last = s_len // POOL_HALO - 1
    body = functools.partial(_pool_body, tm=tm, s_len=s_len)
    return pl.pallas_call(
        body,
        grid=(s_len // tm, 4),
        in_specs=[pl.BlockSpec((POOL_HALO, POOL_GROUP), lambda i, g: (jnp.maximum(i * hb - 1, 0), g)),
                  pl.BlockSpec((tm, POOL_GROUP), lambda i, g: (i, g)),
                  pl.BlockSpec((POOL_HALO, POOL_GROUP), lambda i, g: (jnp.minimum((i + 1) * hb, last), g)),
                  pl.BlockSpec((1, POOL_GROUP, POOL_GROUP), lambda i, g: (g, 0, 0)),
                  pl.BlockSpec((1, POOL_GROUP), lambda i, g: (0, g))],
        out_specs=pl.BlockSpec((tm, POOL_GROUP), lambda i, g: (i, g)),
        out_shape=jax.ShapeDtypeStruct((s_len, D_MODEL), BF16),
        compiler_params=_params(("parallel", "arbitrary"), 32),
    )(proj, proj, proj, pool_w, pool_scale)


def _deepnorm(h, y, gamma, beta):
    z = ALPHA * h + y
    mu = jnp.mean(z, axis=-1, keepdims=True)
    zc = z - mu
    var = jnp.mean(zc * zc, axis=-1, keepdims=True)
    return zc * lax.rsqrt(var + LN_EPS) * gamma + beta


def _deepnorm_rows(h_ref, g_ref, b_ref, o_ref, ob_ref):
    def rows_step(r, carry):
        rows = pl.ds(pl.multiple_of(r * LN_ROWS, LN_ROWS), LN_ROWS)
        out = _deepnorm(h_ref[rows, :], o_ref[rows, :], g_ref[...], b_ref[...])
        o_ref[rows, :] = out
        ob_ref[rows, :] = out.astype(ob_ref.dtype)
        return carry

    lax.fori_loop(0, o_ref.shape[0] // LN_ROWS, rows_step, 0)


def _proj_ln_body(a_ref, w_ref, h_ref, g_ref, b_ref, o_ref, ob_ref, *, nk):
    k = pl.program_id(1)

    @pl.when(k == 0)
    def _():
        o_ref[...] = jnp.zeros_like(o_ref)

    for n in range(0, o_ref.shape[1], MLP_OUT_CHUNK):
        cols = slice(n, n + MLP_OUT_CHUNK)
        o_ref[:, cols] += jnp.dot(a_ref[...], w_ref[:, cols], preferred_element_type=F32)

    @pl.when(k == nk - 1)
    def _():
        _deepnorm_rows(h_ref, g_ref, b_ref, o_ref, ob_ref)


def _proj_ln(a, w, h, gamma, beta, *, tm=512, tk=1024):
    m, kdim = a.shape
    d = w.shape[1]
    tm = min(tm, m)
    row = lambda i, k: (i, 0)
    return pl.pallas_call(
        functools.partial(_proj_ln_body, nk=kdim // tk),
        grid=(m // tm, kdim // tk),
        in_specs=[pl.BlockSpec((tm, tk), lambda i, k: (i, k)),
                  pl.BlockSpec((tk, d), lambda i, k: (k, 0)),
                  pl.BlockSpec((tm, d), row, pipeline_mode=pl.Buffered(1)),
                  pl.BlockSpec((1, d), lambda i, k: (0, 0)),
                  pl.BlockSpec((1, d), lambda i, k: (0, 0))],
        out_specs=[pl.BlockSpec((tm, d), row), pl.BlockSpec((tm, d), row)],
        out_shape=[jax.ShapeDtypeStruct((m, d), F32), jax.ShapeDtypeStruct((m, d), BF16)],
        compiler_params=_params(("parallel", "arbitrary"), 56),
    )(a, w, h, gamma, beta)


def _mlp_body(h_ref, hb_ref, w1_ref, w2_ref, g_ref, b_ref, o_ref, ob_ref, *, nf):
    f = pl.program_id(1)

    @pl.when(f == 0)
    def _():
        o_ref[...] = jnp.zeros_like(o_ref)

    a = jnp.dot(hb_ref[...], w1_ref[...], preferred_element_type=F32)
    a = jnp.square(jnp.maximum(a, 0.0)).astype(BF16)
    for n in range(0, o_ref.shape[1], MLP_OUT_CHUNK):
        cols = slice(n, n + MLP_OUT_CHUNK)
        o_ref[:, cols] += jnp.dot(a, w2_ref[:, cols], preferred_element_type=F32)

    @pl.when(f == nf - 1)
    def _():
        _deepnorm_rows(h_ref, g_ref, b_ref, o_ref, ob_ref)


def _mlp_ln(h, hb, w1, w2, gamma, beta, *, tm=512, tf=512):
    m, d = h.shape
    dff = w1.shape[1]
    tm = min(tm, m)
    body = functools.partial(_mlp_body, nf=dff // tf)
    row = lambda i, f: (i, 0)
    return pl.pallas_call(
        body,
        grid=(m // tm, dff // tf),
        in_specs=[pl.BlockSpec((tm, d), row, pipeline_mode=pl.Buffered(1)),
                  pl.BlockSpec((tm, d), row, pipeline_mode=pl.Buffered(1)),
                  pl.BlockSpec((d, tf), lambda i, f: (0, f)),
                  pl.BlockSpec((tf, d), lambda i, f: (f, 0)),
                  pl.BlockSpec((1, d), lambda i, f: (0, 0)),
                  pl.BlockSpec((1, d), lambda i, f: (0, 0))],
        out_specs=[pl.BlockSpec((tm, d), row), pl.BlockSpec((tm, d), row)],
        out_shape=[jax.ShapeDtypeStruct((m, d), F32), jax.ShapeDtypeStruct((m, d), BF16)],
        compiler_params=_params(("parallel", "arbitrary"), 56),
    )(h, hb, w1, w2, gamma, beta)


def _rope_tables(s_len):
    rows = s_len // GRID_W
    inv_freq = ROPE_THETA ** (-jnp.arange(N_FREQ, dtype=F32) / N_FREQ)
    row_ang = jnp.arange(rows, dtype=F32)[:, None, None] * inv_freq
    col_ang = jnp.arange(GRID_W, dtype=F32)[None, :, None] * inv_freq
    ang = jnp.stack([jnp.broadcast_to(row_ang, (rows, GRID_W, N_FREQ)),
                     jnp.broadcast_to(col_ang, (rows, GRID_W, N_FREQ))], axis=2).reshape(s_len, 2, 1, N_FREQ)
    cos_t = jnp.broadcast_to(jnp.cos(ang), (s_len, 2, 2, N_FREQ)).reshape(s_len, HEAD_DIM)
    sign = jnp.array([-1.0, 1.0], F32).reshape(1, 1, 2, 1)
    sin_t = (jnp.sin(ang) * sign).reshape(s_len, HEAD_DIM)
    return cos_t, sin_t


def _vec(p):
    return p.reshape(1, -1).astype(F32)


def kernel(x, mem, w_mem_kv, l0_w_in, l0_q_gain, l0_k_gain, l0_w_out, l0_ln1_g, l0_ln1_b, l0_w_ff1, l0_w_ff2, l0_ln2_g, l0_ln2_b, l1_w_in, l1_pool_w, l1_pool_scale, l1_w_out, l1_ln1_g, l1_ln1_b, l1_w_ff1, l1_w_ff2, l1_ln2_g, l1_ln2_b, l2_w_in, l2_q_gain, l2_k_gain, l2_w_out, l2_ln1_g, l2_ln1_b, l2_w_ff1, l2_w_ff2, l2_ln2_g, l2_ln2_b, l3_w_in, l3_pool_w, l3_pool_scale, l3_w_out, l3_ln1_g, l3_ln1_b, l3_w_ff1, l3_w_ff2, l3_ln2_g, l3_ln2_b):
    layers = [
        (l0_w_in, l0_q_gain, l0_k_gain, l0_w_out, l0_ln1_g, l0_ln1_b, l0_w_ff1, l0_w_ff2, l0_ln2_g, l0_ln2_b),
        (l1_w_in, l1_pool_w, l1_pool_scale, l1_w_out, l1_ln1_g, l1_ln1_b, l1_w_ff1, l1_w_ff2, l1_ln2_g, l1_ln2_b),
        (l2_w_in, l2_q_gain, l2_k_gain, l2_w_out, l2_ln1_g, l2_ln1_b, l2_w_ff1, l2_w_ff2, l2_ln2_g, l2_ln2_b),
        (l3_w_in, l3_pool_w, l3_pool_scale, l3_w_out, l3_ln1_g, l3_ln1_b, l3_w_ff1, l3_w_ff2, l3_ln2_g, l3_ln2_b),
    ]
    b, s_len, d = x.shape
    assert b == 1 and d == D_MODEL

    mem_kv = _matmul(mem[0].astype(BF16), w_mem_kv.astype(BF16),
                     col0=0, ncols=2 * MEM_WIDTH, tn=MEM_WIDTH, out_dtype=BF16)
    cos_t, sin_t = _rope_tables(s_len)
    q_scale = HEAD_DIM ** -0.5 * math.log2(math.e)
    cos_q, sin_q = cos_t * q_scale, sin_t * q_scale

    h = x[0]
    hb = h.astype(BF16)
    for i in range(DEPTH):
        w_in, pa, pb, w_out, ln1_g, ln1_b, w_ff1, w_ff2, ln2_g, ln2_b = layers[i]
        w_in = w_in.astype(BF16)
        if i % 2 == 0:
            qt = _qk_proj(hb, w_in, _vec(pa), cos_q, sin_q, col0=0, ncols=MIXER_WIDTH, tn=1024,
                          transpose_out=True)
            k = _qk_proj(hb, w_in, _vec(pb), cos_t, sin_t, col0=MIXER_WIDTH, ncols=KV_WIDTH, tn=KV_WIDTH)
            vt = _v_proj(hb, w_in, col0=MIXER_WIDTH + KV_WIDTH, tk=min(ATTN_KEY_CHUNK, s_len // 2))
            qm = _matmul(hb, w_in, col0=MIXER_WIDTH + 2 * KV_WIDTH, ncols=MEM_WIDTH, tn=512, out_dtype=BF16)
            y = _attention(qt, k, vt)
            y = _mem_attention(qm, 0, mem_kv, y)
        else:
            proj = _matmul(hb, w_in, col0=0, ncols=D_MODEL, tn=1024, out_dtype=BF16)
            y = _pool_mix(proj, pa.astype(BF16), _vec(pb))
            y = _mem_attention(proj, MIXER_WIDTH // MEM_WIDTH, mem_kv, y)
        h, hb = _proj_ln(y, w_out.astype(BF16), h, _vec(ln1_g), _vec(ln1_b))
        h, hb = _mlp_ln(h, hb, w_ff1.astype(BF16), w_ff2.astype(BF16), _vec(ln2_g), _vec(ln2_b))
    return h[None]
```

```python
import functools
import math

import jax
import jax.numpy as jnp
from jax import lax
from jax.experimental import pallas as pl
from jax.experimental.pallas import tpu as pltpu

D_MODEL = 4096
DEPTH = 4
GRID_W = 64
N_MEM = 256
MEM_HEADS = 4
MEM_HEAD_DIM = 256
MEM_WIDTH = MEM_HEADS * MEM_HEAD_DIM
MIXER_WIDTH = D_MODEL - MEM_WIDTH
HEAD_DIM = 128
N_KV_HEADS = 6
Q_PER_KV = 4
GROUP_WIDTH = Q_PER_KV * HEAD_DIM
KV_WIDTH = N_KV_HEADS * HEAD_DIM
DENOM_ROWS = 16
ATTN_KEY_CHUNK = 512
SCORE_BOUND_LIMIT = 64.0
ROPE_THETA = 10000.0
N_FREQ = HEAD_DIM // 4
POOL_GROUP = MIXER_WIDTH // 4
POOL_HALO = 16
D_FF = 4 * D_MODEL
MLP_OUT_CHUNK = 512
LN_ROWS = 64
ALPHA = (2 * DEPTH) ** 0.25
LN_EPS = 1e-5
RMS_EPS = 1e-6

MIB = 1024 * 1024
BF16 = jnp.bfloat16
F32 = jnp.float32
NT_DIMS = (((1,), (1,)), ((), ()))


def _params(semantics, vmem_mib):
    return pltpu.CompilerParams(dimension_semantics=semantics, vmem_limit_bytes=vmem_mib * MIB)


def _mm_body(a_ref, b_ref, o_ref):
    o_ref[...] = jnp.dot(a_ref[...], b_ref[...], preferred_element_type=F32).astype(o_ref.dtype)


def _matmul(a, b, *, col0, ncols, tn, out_dtype, tm=1024):
    m, k = a.shape
    tm = min(tm, m)
    nb0 = col0 // tn
    assert col0 % tn == 0 and ncols % tn == 0 and m % tm == 0
    return pl.pallas_call(
        _mm_body,
        grid=(m // tm, ncols // tn),
        in_specs=[pl.BlockSpec((tm, k), lambda i, j: (i, 0)),
                  pl.BlockSpec((k, tn), lambda i, j: (0, j + nb0))],
        out_specs=pl.BlockSpec((tm, tn), lambda i, j: (i, j)),
        out_shape=jax.ShapeDtypeStruct((m, ncols), out_dtype),
        compiler_params=_params(("parallel", "arbitrary"), 48),
    )(a, b)


def _v_body(a_ref, b_ref, o_ref, *, tk):
    acc_t = jnp.dot(a_ref[...], b_ref[...], preferred_element_type=F32).T
    ones = jnp.ones((DENOM_ROWS, tk), o_ref.dtype)
    for h in range(N_KV_HEADS):
        for c in range(o_ref.shape[1]):
            o_ref[h, c, :HEAD_DIM, :] = acc_t[h * HEAD_DIM:(h + 1) * HEAD_DIM,
                                              c * tk:(c + 1) * tk].astype(o_ref.dtype)
            o_ref[h, c, HEAD_DIM:, :] = ones


def _v_proj(a, b, *, col0, tk, tm=1024):
    m, k = a.shape
    tm = min(tm, m)
    assert col0 % KV_WIDTH == 0 and m % tm == 0 and tm % tk == 0
    nb0 = col0 // KV_WIDTH
    return pl.pallas_call(
        functools.partial(_v_body, tk=tk),
        grid=(m // tm,),
        in_specs=[pl.BlockSpec((tm, k), lambda i: (i, 0)),
                  pl.BlockSpec((k, KV_WIDTH), lambda i: (0, nb0))],
        out_specs=pl.BlockSpec((N_KV_HEADS, tm // tk, HEAD_DIM + DENOM_ROWS, tk), lambda i: (0, i, 0, 0)),
        out_shape=jax.ShapeDtypeStruct((N_KV_HEADS, m // tk, HEAD_DIM + DENOM_ROWS, tk), BF16),
        compiler_params=_params(("parallel",), 48),
    )(a, b)


def _qk_body(a_ref, b_ref, gain_ref, cos_ref, sin_ref, o_ref, *, transpose_out):
    acc = jnp.dot(a_ref[...], b_ref[...], preferred_element_type=F32)
    tm, tn = acc.shape
    lane = lax.broadcasted_iota(jnp.int32, (tm, HEAD_DIM), 1)
    first_half = (lane & (N_FREQ)) == 0
    gain = gain_ref[...]
    cos = cos_ref[...]
    sin = sin_ref[...]
    for j in range(tn // HEAD_DIM):
        x = acc[:, j * HEAD_DIM:(j + 1) * HEAD_DIM]
        ms = jnp.mean(x * x, axis=-1, keepdims=True)
        xn = x * lax.rsqrt(ms + RMS_EPS) * gain
        partner = jnp.where(first_half,
                            pltpu.roll(xn, HEAD_DIM - N_FREQ, 1),
                            pltpu.roll(xn, N_FREQ, 1))
        out = xn * cos + partner * sin
        if transpose_out:
            o_ref[j * HEAD_DIM:(j + 1) * HEAD_DIM, :] = out.T.astype(o_ref.dtype)
        else:
            o_ref[:, j * HEAD_DIM:(j + 1) * HEAD_DIM] = out.astype(o_ref.dtype)


def _qk_proj(a, b, gain, cos_t, sin_t, *, col0, ncols, tn, tm=1024, transpose_out=False):
    m, k = a.shape
    tm = min(tm, m)
    nb0 = col0 // tn
    assert col0 % tn == 0 and ncols % tn == 0 and m % tm == 0
    if transpose_out:
        out_spec = pl.BlockSpec((tn, tm), lambda i, j: (j, i))
        out_shape = jax.ShapeDtypeStruct((ncols, m), BF16)
    else:
        out_spec = pl.BlockSpec((tm, tn), lambda i, j: (i, j))
        out_shape = jax.ShapeDtypeStruct((m, ncols), BF16)
    return pl.pallas_call(
        functools.partial(_qk_body, transpose_out=transpose_out),
        grid=(m // tm, ncols // tn),
        in_specs=[pl.BlockSpec((tm, k), lambda i, j: (i, 0)),
                  pl.BlockSpec((k, tn), lambda i, j: (0, j + nb0)),
                  pl.BlockSpec((1, HEAD_DIM), lambda i, j: (0, 0)),
                  pl.BlockSpec((tm, HEAD_DIM), lambda i, j: (i, 0)),
                  pl.BlockSpec((tm, HEAD_DIM), lambda i, j: (i, 0))],
        out_specs=out_spec,
        out_shape=out_shape,
        compiler_params=_params(("parallel", "arbitrary"), 56),
    )(a, b, gain, cos_t, sin_t)


def _attn_body(qt_ref, k_ref, vt_ref, o_ref, st_a, st_b, mx_a, mx_b, m_ref, acc_ref, *, tk, nk):
    def scores(kc, g, dst, mx_dst):
        st = jnp.dot(kc, qt_ref[g * HEAD_DIM:(g + 1) * HEAD_DIM, :],
                     preferred_element_type=F32)
        dst[g] = st
        mx_dst[g] = jnp.max(st, axis=0, keepdims=True)

    def key_chunk(c):
        return k_ref[pl.ds(pl.multiple_of(c * tk, tk), tk), :]

    def chunk(c, src, mx_src, dst, mx_dst):
        kc_next = key_chunk(jnp.minimum(c + 1, nk - 1))
        vtc = vt_ref[0, c]
        for g in range(Q_PER_KV):
            scores(kc_next, g, dst, mx_dst)
            m = m_ref[g]
            m_new = jnp.maximum(m, mx_src[g])
            alpha = jnp.exp2(m - m_new)
            pt = jnp.exp2(src[g] - m_new).astype(BF16)
            m_ref[g] = m_new
            acc_ref[g] = alpha * acc_ref[g] + jnp.dot(vtc, pt,
                                                      preferred_element_type=F32)

    m_ref[...] = jnp.full(m_ref.shape, -jnp.inf, F32)
    acc_ref[...] = jnp.zeros(acc_ref.shape, F32)
    for g in range(Q_PER_KV):
        scores(key_chunk(0), g, st_a, mx_a)

    def pair(j, carry):
        chunk(2 * j, st_a, mx_a, st_b, mx_b)
        chunk(2 * j + 1, st_b, mx_b, st_a, mx_a)
        return carry

    lax.fori_loop(0, nk // 2, pair, 0)
    for g in range(Q_PER_KV):
        denom = acc_ref[g, HEAD_DIM:HEAD_DIM + 1, :]
        out_t = acc_ref[g, :HEAD_DIM, :] * (1.0 / denom)
        o_ref[:, g * HEAD_DIM:(g + 1) * HEAD_DIM] = out_t.T.astype(o_ref.dtype)


def _attn_bounded_body(qt_ref, k_ref, vt_ref, o_ref, acc_ref, *, tk, nk):
    acc_ref[...] = jnp.zeros(acc_ref.shape, F32)

    def pair(j, carry):
        pending = None
        for u in range(2 * Q_PER_KV):
            c, g = 2 * j + u // Q_PER_KV, u % Q_PER_KV
            kc = k_ref[pl.ds(pl.multiple_of(c * tk, tk), tk), :]
            st = jnp.dot(kc, qt_ref[g * HEAD_DIM:(g + 1) * HEAD_DIM, :],
                         preferred_element_type=F32)
            if pending is not None:
                pc, pg, pt = pending
                acc_ref[pg] += jnp.dot(vt_ref[0, pc], pt, preferred_element_type=F32)
            pending = (c, g, jnp.exp2(st).astype(BF16))
        pc, pg, pt = pending
        acc_ref[pg] += jnp.dot(vt_ref[0, pc], pt, preferred_element_type=F32)
        return carry

    lax.fori_loop(0, nk // 2, pair, 0)
    for g in range(Q_PER_KV):
        denom = acc_ref[g, HEAD_DIM:HEAD_DIM + 1, :]
        out_t = acc_ref[g, :HEAD_DIM, :] * (1.0 / denom)
        o_ref[:, g * HEAD_DIM:(g + 1) * HEAD_DIM] = out_t.T.astype(o_ref.dtype)


def _attention(qt, k, vt, *, bounded, tq=512):
    s_len = k.shape[0]
    nk, tk = vt.shape[1], vt.shape[3]
    tq = min(tq, s_len)
    assert nk * tk == s_len and nk % 2 == 0 and s_len % tq == 0
    acc = pltpu.VMEM((Q_PER_KV, HEAD_DIM + DENOM_ROWS, tq), F32)
    if bounded:
        body = functools.partial(_attn_bounded_body, tk=tk, nk=nk)
        scratch = [acc]
    else:
        body = functools.partial(_attn_body, tk=tk, nk=nk)
        scratch = [pltpu.VMEM((Q_PER_KV, tk, tq), F32),
                   pltpu.VMEM((Q_PER_KV, tk, tq), F32),
                   pltpu.VMEM((Q_PER_KV, 1, tq), F32),
                   pltpu.VMEM((Q_PER_KV, 1, tq), F32),
                   pltpu.VMEM((Q_PER_KV, 1, tq), F32),
                   acc]
    return pl.pallas_call(
        body,
        grid=(N_KV_HEADS, s_len // tq),
        in_specs=[pl.BlockSpec((GROUP_WIDTH, tq), lambda h, i: (h, i)),
                  pl.BlockSpec((s_len, HEAD_DIM), lambda h, i: (0, h)),
                  pl.BlockSpec((1, nk, HEAD_DIM + DENOM_ROWS, tk), lambda h, i: (h, 0, 0, 0))],
        out_specs=pl.BlockSpec((tq, GROUP_WIDTH), lambda h, i: (i, h)),
        out_shape=jax.ShapeDtypeStruct((s_len, D_MODEL), BF16),
        scratch_shapes=scratch,
        compiler_params=_params(("parallel", "arbitrary"), 48),
    )(qt, k, vt)


def _score_bound(q_gain, k_gain, q_scale):
    return 1.02 * HEAD_DIM * q_scale * jnp.max(jnp.abs(q_gain)) * jnp.max(jnp.abs(k_gain))


def _mem_body(qm_ref, mk_ref, mv_ref, y_hbm_ref, o_ref):
    del y_hbm_ref
    scale = MEM_HEAD_DIM ** -0.5
    for h in range(MEM_HEADS):
        cols = slice(h * MEM_HEAD_DIM, (h + 1) * MEM_HEAD_DIM)
        s = lax.dot_general(qm_ref[:, cols], mk_ref[:, cols], NT_DIMS,
                            preferred_element_type=F32) * scale
        p = jnp.exp(s - jnp.max(s, axis=-1, keepdims=True))
        p = p * (1.0 / jnp.sum(p, axis=-1, keepdims=True))
        o_ref[:, cols] = jnp.dot(p.astype(BF16), mv_ref[:, cols],
                                 preferred_element_type=F32).astype(o_ref.dtype)


def _mem_attention(qm_arr, qm_block, mem_kv, y, *, tm=1024):
    s_len = y.shape[0]
    tm = min(tm, s_len)
    out_block = MIXER_WIDTH // MEM_WIDTH
    return pl.pallas_call(
        _mem_body,
        grid=(s_len // tm,),
        in_specs=[pl.BlockSpec((tm, MEM_WIDTH), lambda i: (i, qm_block)),
                  pl.BlockSpec((N_MEM, MEM_WIDTH), lambda i: (0, 0)),
                  pl.BlockSpec((N_MEM, MEM_WIDTH), lambda i: (0, 1)),
                  pl.BlockSpec(memory_space=pl.ANY)],
        out_specs=pl.BlockSpec((tm, MEM_WIDTH), lambda i: (i, out_block)),
        out_shape=jax.ShapeDtypeStruct(y.shape, y.dtype),
        input_output_aliases={3: 0},
        compiler_params=_params(("parallel",), 32),
    )(qm_arr, mem_kv, mem_kv, y)


def _pool_body(up_ref, um_ref, un_ref, pw_ref, sc_ref, o_ref, *, tm, s_len):
    i = pl.program_id(0)
    g = pl.program_id(1)
    lo = jnp.left_shift(jnp.int32(1), g)
    hi = lo - 1
    u_ext = jnp.concatenate([up_ref[...], um_ref[...], un_ref[...]], axis=0)
    ext = tm + 2 * POOL_HALO
    t = lax.broadcasted_iota(jnp.int32, (tm, ext), 0)
    j = lax.broadcasted_iota(jnp.int32, (tm, ext), 1)
    d = j - POOL_HALO - t
    r = i * tm + j - POOL_HALO
    in_window = jnp.where(d >= -lo, 1.0, 0.0)
    in_window = jnp.where(d <= hi, in_window, 0.0)
    in_window = jnp.where(r >= 0, in_window, 0.0)
    in_window = jnp.where(r < s_len, in_window, 0.0)
    t_col = i * tm + lax.broadcasted_iota(jnp.int32, (tm, 1), 0)
    cnt = jnp.minimum(t_col + hi + 1, s_len) - jnp.maximum(t_col - lo, 0)
    cnt_f = cnt.astype(F32)
    band = jnp.where(d == 0, 1.0 - cnt_f, in_window).astype(BF16)
    pooled = jnp.dot(band, u_ext, preferred_element_type=F32) * (1.0 / cnt_f)
    y = jnp.dot(pooled.astype(BF16), pw_ref[0], preferred_element_type=F32) * sc_ref[...]
    o_ref[...] = y.astype(o_ref.dtype)


def _pool_mix(proj, pool_w, pool_scale, *, tm=512):
    s_len = proj.shape[0]
    tm = min(tm, s_len)
    hb = tm // POOL_HALO
    last = s_len // POOL_HALO - 1
    body = functools.partial(_pool_body, tm=tm, s_len=s_len)
    return pl.pallas_call(
        body,
        grid=(s_len // tm, 4),
        in_specs=[pl.BlockSpec((POOL_HALO, POOL_GROUP), lambda i, g: (jnp.maximum(i * hb - 1, 0), g)),
                  pl.BlockSpec((tm, POOL_GROUP), lambda i, g: (i, g)),
                  pl.BlockSpec((POOL_HALO, POOL_GROUP), lambda i, g: (jnp.minimum((i + 1) * hb, last), g)),
                  pl.BlockSpec((1, POOL_GROUP, POOL_GROUP), lambda i, g: (g, 0, 0)),
                  pl.BlockSpec((1, POOL_GROUP), lambda i, g: (0, g))],
        out_specs=pl.BlockSpec((tm, POOL_GROUP), lambda i, g: (i, g)),
        out_shape=jax.ShapeDtypeStruct((s_len, D_MODEL), BF16),
        compiler_params=_params(("parallel", "arbitrary"), 32),
    )(proj, proj, proj, pool_w, pool_scale)


def _deepnorm(h, y, gamma, beta):
    z = ALPHA * h + y
    mu = jnp.mean(z, axis=-1, keepdims=True)
    zc = z - mu
    var = jnp.mean(zc * zc, axis=-1, keepdims=True)
    return zc * lax.rsqrt(var + LN_EPS) * gamma + beta


def _deepnorm_rows(h_ref, g_ref, b_ref, o_ref, ob_ref):
    def rows_step(r, carry):
        rows = pl.ds(pl.multiple_of(r * LN_ROWS, LN_ROWS), LN_ROWS)
        out = _deepnorm(h_ref[rows, :], o_ref[rows, :], g_ref[...], b_ref[...])
        o_ref[rows, :] = out
        ob_ref[rows, :] = out.astype(ob_ref.dtype)
        return carry

    lax.fori_loop(0, o_ref.shape[0] // LN_ROWS, rows_step, 0)


def _ln_body(y_ref, h_ref, g_ref, b_ref, o_ref, ob_ref):
    out = _deepnorm(h_ref[...], y_ref[...], g_ref[...], b_ref[...])
    o_ref[...] = out
    ob_ref[...] = out.astype(ob_ref.dtype)


def _residual_ln(y, h, gamma, beta, *, tm=128):
    m, d = h.shape
    tm = min(tm, m)
    row = pl.BlockSpec((tm, d), lambda i: (i, 0))
    vec = pl.BlockSpec((1, d), lambda i: (0, 0))
    return pl.pallas_call(
        _ln_body,
        grid=(m // tm,),
        in_specs=[row, row, vec, vec],
        out_specs=[row, row],
        out_shape=[jax.ShapeDtypeStruct((m, d), F32), jax.ShapeDtypeStruct((m, d), BF16)],
        compiler_params=_params(("parallel",), 32),
    )(y, h, gamma, beta)


def _mlp_body(h_ref, hb_ref, w1_ref, w2_ref, g_ref, b_ref, o_ref, ob_ref, *, nf):
    f = pl.program_id(1)

    @pl.when(f == 0)
    def _():
        o_ref[...] = jnp.zeros_like(o_ref)

    a = jnp.dot(hb_ref[...], w1_ref[...], preferred_element_type=F32)
    a = jnp.square(jnp.maximum(a, 0.0)).astype(BF16)
    for n in range(0, o_ref.shape[1], MLP_OUT_CHUNK):
        cols = slice(n, n + MLP_OUT_CHUNK)
        o_ref[:, cols] += jnp.dot(a, w2_ref[:, cols], preferred_element_type=F32)

    @pl.when(f == nf - 1)
    def _():
        _deepnorm_rows(h_ref, g_ref, b_ref, o_ref, ob_ref)


def _mlp_ln(h, hb, w1, w2, gamma, beta, *, tm=512, tf=512):
    m, d = h.shape
    dff = w1.shape[1]
    tm = min(tm, m)
    body = functools.partial(_mlp_body, nf=dff // tf)
    row = lambda i, f: (i, 0)
    return pl.pallas_call(
        body,
        grid=(m // tm, dff // tf),
        in_specs=[pl.BlockSpec((tm, d), row, pipeline_mode=pl.Buffered(1)),
                  pl.BlockSpec((tm, d), row, pipeline_mode=pl.Buffered(1)),
                  pl.BlockSpec((d, tf), lambda i, f: (0, f)),
                  pl.BlockSpec((tf, d), lambda i, f: (f, 0)),
                  pl.BlockSpec((1, d), lambda i, f: (0, 0)),
                  pl.BlockSpec((1, d), lambda i, f: (0, 0))],
        out_specs=[pl.BlockSpec((tm, d), row), pl.BlockSpec((tm, d), row)],
        out_shape=[jax.ShapeDtypeStruct((m, d), F32), jax.ShapeDtypeStruct((m, d), BF16)],
        compiler_params=_params(("parallel", "arbitrary"), 56),
    )(h, hb, w1, w2, gamma, beta)


def _rope_tables(s_len):
    rows = s_len // GRID_W
    inv_freq = ROPE_THETA ** (-jnp.arange(N_FREQ, dtype=F32) / N_FREQ)
    row_ang = jnp.arange(rows, dtype=F32)[:, None, None] * inv_freq
    col_ang = jnp.arange(GRID_W, dtype=F32)[None, :, None] * inv_freq
    ang = jnp.stack([jnp.broadcast_to(row_ang, (rows, GRID_W, N_FREQ)),
                     jnp.broadcast_to(col_ang, (rows, GRID_W, N_FREQ))], axis=2).reshape(s_len, 2, 1, N_FREQ)
    cos_t = jnp.broadcast_to(jnp.cos(ang), (s_len, 2, 2, N_FREQ)).reshape(s_len, HEAD_DIM)
    sign = jnp.array([-1.0, 1.0], F32).reshape(1, 1, 2, 1)
    sin_t = (jnp.sin(ang) * sign).reshape(s_len, HEAD_DIM)
    return cos_t, sin_t


def _vec(p):
    return p.reshape(1, -1).astype(F32)


def kernel(x, mem, w_mem_kv, l0_w_in, l0_q_gain, l0_k_gain, l0_w_out, l0_ln1_g, l0_ln1_b, l0_w_ff1, l0_w_ff2, l0_ln2_g, l0_ln2_b, l1_w_in, l1_pool_w, l1_pool_scale, l1_w_out, l1_ln1_g, l1_ln1_b, l1_w_ff1, l1_w_ff2, l1_ln2_g, l1_ln2_b, l2_w_in, l2_q_gain, l2_k_gain, l2_w_out, l2_ln1_g, l2_ln1_b, l2_w_ff1, l2_w_ff2, l2_ln2_g, l2_ln2_b, l3_w_in, l3_pool_w, l3_pool_scale, l3_w_out, l3_ln1_g, l3_ln1_b, l3_w_ff1, l3_w_ff2, l3_ln2_g, l3_ln2_b):
    layers = [
        (l0_w_in, l0_q_gain, l0_k_gain, l0_w_out, l0_ln1_g, l0_ln1_b, l0_w_ff1, l0_w_ff2, l0_ln2_g, l0_ln2_b),
        (l1_w_in, l1_pool_w, l1_pool_scale, l1_w_out, l1_ln1_g, l1_ln1_b, l1_w_ff1, l1_w_ff2, l1_ln2_g, l1_ln2_b),
        (l2_w_in, l2_q_gain, l2_k_gain, l2_w_out, l2_ln1_g, l2_ln1_b, l2_w_ff1, l2_w_ff2, l2_ln2_g, l2_ln2_b),
        (l3_w_in, l3_pool_w, l3_pool_scale, l3_w_out, l3_ln1_g, l3_ln1_b, l3_w_ff1, l3_w_ff2, l3_ln2_g, l3_ln2_b),
    ]
    b, s_len, d = x.shape
    assert b == 1 and d == D_MODEL

    mem_kv = _matmul(mem[0].astype(BF16), w_mem_kv.astype(BF16),
                     col0=0, ncols=2 * MEM_WIDTH, tn=MEM_WIDTH, out_dtype=BF16)
    cos_t, sin_t = _rope_tables(s_len)
    q_scale = HEAD_DIM ** -0.5 * math.log2(math.e)
    cos_q, sin_q = cos_t * q_scale, sin_t * q_scale

    h = x[0]
    hb = h.astype(BF16)
    for i in range(DEPTH):
        w_in, pa, pb, w_out, ln1_g, ln1_b, w_ff1, w_ff2, ln2_g, ln2_b = layers[i]
        w_in = w_in.astype(BF16)
        if i % 2 == 0:
            qt = _qk_proj(hb, w_in, _vec(pa), cos_q, sin_q, col0=0, ncols=MIXER_WIDTH, tn=1024,
                          transpose_out=True)
            k = _qk_proj(hb, w_in, _vec(pb), cos_t, sin_t, col0=MIXER_WIDTH, ncols=KV_WIDTH, tn=KV_WIDTH)
            vt = _v_proj(hb, w_in, col0=MIXER_WIDTH + KV_WIDTH, tk=min(ATTN_KEY_CHUNK, s_len // 2))
            qm = _matmul(hb, w_in, col0=MIXER_WIDTH + 2 * KV_WIDTH, ncols=MEM_WIDTH, tn=512, out_dtype=BF16)
            y = lax.cond(_score_bound(pa, pb, q_scale) <= SCORE_BOUND_LIMIT,
                         functools.partial(_attention, bounded=True),
                         functools.partial(_attention, bounded=False),
                         qt, k, vt)
            y = _mem_attention(qm, 0, mem_kv, y)
        else:
            proj = _matmul(hb, w_in, col0=0, ncols=D_MODEL, tn=1024, out_dtype=BF16)
            y = _pool_mix(proj, pa.astype(BF16), _vec(pb))
            y = _mem_attention(proj, MIXER_WIDTH // MEM_WIDTH, mem_kv, y)
        y = _matmul(y, w_out.astype(BF16), col0=0, ncols=D_MODEL, tn=1024, out_dtype=F32)
        h, hb = _residual_ln(y, h, _vec(ln1_g), _vec(ln1_b))
        h, hb = _mlp_ln(h, hb, w_ff1.astype(BF16), w_ff2.astype(BF16), _vec(ln2_g), _vec(ln2_b))
    return h[None]
```

```python
import functools
import math

import jax
import jax.numpy as jnp
from jax import lax
from jax.experimental import pallas as pl
from jax.experimental.pallas import tpu as pltpu

D_MODEL = 4096
DEPTH = 4
GRID_W = 64
N_MEM = 256
MEM_HEADS = 4
MEM_HEAD_DIM = 256
MEM_WIDTH = MEM_HEADS * MEM_HEAD_DIM
MIXER_WIDTH = D_MODEL - MEM_WIDTH
HEAD_DIM = 128
N_KV_HEADS = 6
Q_PER_KV = 4
GROUP_WIDTH = Q_PER_KV * HEAD_DIM
KV_WIDTH = N_KV_HEADS * HEAD_DIM
ATTN_KEY_CHUNK = 512
SCORE_BOUND_LIMIT = 64.0
ROPE_THETA = 10000.0
N_FREQ = HEAD_DIM // 4
POOL_GROUP = MIXER_WIDTH // 4
POOL_HALO = 16
D_FF = 4 * D_MODEL
MLP_OUT_CHUNK = 512
LN_ROWS = 64
ALPHA = (2 * DEPTH) ** 0.25
LN_EPS = 1e-5
RMS_EPS = 1e-6

MIB = 1024 * 1024
BF16 = jnp.bfloat16
F32 = jnp.float32
NT_DIMS = (((1,), (1,)), ((), ()))


def _params(semantics, vmem_mib):
    return pltpu.CompilerParams(dimension_semantics=semantics, vmem_limit_bytes=vmem_mib * MIB)


def _mm_body(a_ref, b_ref, o_ref):
    o_ref[...] = jnp.dot(a_ref[...], b_ref[...], preferred_element_type=F32).astype(o_ref.dtype)


def _matmul(a, b, *, col0, ncols, tn, out_dtype, tm=1024):
    m, k = a.shape
    tm = min(tm, m)
    nb0 = col0 // tn
    assert col0 % tn == 0 and ncols % tn == 0 and m % tm == 0
    return pl.pallas_call(
        _mm_body,
        grid=(m // tm, ncols // tn),
        in_specs=[pl.BlockSpec((tm, k), lambda i, j: (i, 0)),
                  pl.BlockSpec((k, tn), lambda i, j: (0, j + nb0))],
        out_specs=pl.BlockSpec((tm, tn), lambda i, j: (i, j)),
        out_shape=jax.ShapeDtypeStruct((m, ncols), out_dtype),
        compiler_params=_params(("parallel", "arbitrary"), 48),
    )(a, b)


def _v_body(a_ref, b_ref, o_ref, *, tk):
    acc_t = jnp.dot(a_ref[...], b_ref[...], preferred_element_type=F32).T
    for h in range(N_KV_HEADS):
        for c in range(o_ref.shape[1]):
            o_ref[h, c] = acc_t[h * HEAD_DIM:(h + 1) * HEAD_DIM,
                                c * tk:(c + 1) * tk].astype(o_ref.dtype)


def _v_proj(a, b, *, col0, tk, tm=1024):
    m, k = a.shape
    tm = min(tm, m)
    assert col0 % KV_WIDTH == 0 and m % tm == 0 and tm % tk == 0
    nb0 = col0 // KV_WIDTH
    return pl.pallas_call(
        functools.partial(_v_body, tk=tk),
        grid=(m // tm,),
        in_specs=[pl.BlockSpec((tm, k), lambda i: (i, 0)),
                  pl.BlockSpec((k, KV_WIDTH), lambda i: (0, nb0))],
        out_specs=pl.BlockSpec((N_KV_HEADS, tm // tk, HEAD_DIM, tk), lambda i: (0, i, 0, 0)),
        out_shape=jax.ShapeDtypeStruct((N_KV_HEADS, m // tk, HEAD_DIM, tk), BF16),
        compiler_params=_params(("parallel",), 48),
    )(a, b)


def _qk_body(a_ref, b_ref, gain_ref, cos_ref, sin_ref, o_ref, raw_a, raw_b, *, transpose_out):
    s = pl.program_id(0)

    def epilogue(raw_ref):
        tm, tn = raw_ref.shape
        lane = lax.broadcasted_iota(jnp.int32, (tm, HEAD_DIM), 1)
        first_half = (lane & (N_FREQ)) == 0
        gain = gain_ref[...]
        cos = cos_ref[...]
        sin = sin_ref[...]
        for j in range(tn // HEAD_DIM):
            x = raw_ref[:, j * HEAD_DIM:(j + 1) * HEAD_DIM]
            ms = jnp.mean(x * x, axis=-1, keepdims=True)
            xn = x * lax.rsqrt(ms + RMS_EPS) * gain
            partner = jnp.where(first_half,
                                pltpu.roll(xn, HEAD_DIM - N_FREQ, 1),
                                pltpu.roll(xn, N_FREQ, 1))
            out = xn * cos + partner * sin
            if transpose_out:
                o_ref[j * HEAD_DIM:(j + 1) * HEAD_DIM, :] = out.T.astype(o_ref.dtype)
            else:
                o_ref[:, j * HEAD_DIM:(j + 1) * HEAD_DIM] = out.astype(o_ref.dtype)

    def step(raw_dst, raw_src):
        raw_dst[...] = jnp.dot(a_ref[...], b_ref[...], preferred_element_type=F32)
        epilogue(raw_src)

    @pl.when(s == 0)
    def _():
        raw_b[...] = jnp.zeros_like(raw_b)

    @pl.when(s % 2 == 0)
    def _():
        step(raw_a, raw_b)

    @pl.when(s % 2 == 1)
    def _():
        step(raw_b, raw_a)


def _qk_proj(a, b, gain, cos_t, sin_t, *, col0, ncols, tn, tm=1024, transpose_out=False):
    m, k = a.shape
    tm = min(tm, m)
    nb0 = col0 // tn
    assert col0 % tn == 0 and ncols % tn == 0 and m % tm == 0
    nj = ncols // tn
    tiles = (m // tm) * nj

    def mm_tile(s):
        return jnp.minimum(s, tiles - 1)

    def ep_tile(s):
        return jnp.maximum(s - 1, 0)

    if transpose_out:
        out_spec = pl.BlockSpec((tn, tm), lambda s: (ep_tile(s) % nj, ep_tile(s) // nj))
        out_shape = jax.ShapeDtypeStruct((ncols, m), BF16)
    else:
        out_spec = pl.BlockSpec((tm, tn), lambda s: (ep_tile(s) // nj, ep_tile(s) % nj))
        out_shape = jax.ShapeDtypeStruct((m, ncols), BF16)
    table = pl.BlockSpec((tm, HEAD_DIM), lambda s: (ep_tile(s) // nj, 0))
    return pl.pallas_call(
        functools.partial(_qk_body, transpose_out=transpose_out),
        grid=(tiles + 1,),
        in_specs=[pl.BlockSpec((tm, k), lambda s: (mm_tile(s) // nj, 0)),
                  pl.BlockSpec((k, tn), lambda s: (0, mm_tile(s) % nj + nb0)),
                  pl.BlockSpec((1, HEAD_DIM), lambda s: (0, 0)),
                  table, table],
        out_specs=out_spec,
        out_shape=out_shape,
        scratch_shapes=[pltpu.VMEM((tm, tn), F32), pltpu.VMEM((tm, tn), F32)],
        compiler_params=_params(("arbitrary",), 56),
    )(a, b, gain, cos_t, sin_t)


def _attn_body(qt_ref, k_ref, vt_ref, o_ref, st_a, st_b, mx_a, mx_b, m_ref, l_ref, acc_ref, *, tk, nk):
    def scores(kc, g, dst, mx_dst):
        st = jnp.dot(kc, qt_ref[g * HEAD_DIM:(g + 1) * HEAD_DIM, :],
                     preferred_element_type=F32)
        dst[g] = st
        mx_dst[g] = jnp.max(st, axis=0, keepdims=True)

    def key_chunk(c):
        return k_ref[pl.ds(pl.multiple_of(c * tk, tk), tk), :]

    def chunk(c, src, mx_src, dst, mx_dst):
        kc_next = key_chunk(jnp.minimum(c + 1, nk - 1))
        vtc = vt_ref[0, c]
        for g in range(Q_PER_KV):
            scores(kc_next, g, dst, mx_dst)
            m = m_ref[g]
            m_new = jnp.maximum(m, mx_src[g])
            alpha = jnp.exp2(m - m_new)
            pt = jnp.exp2(src[g] - m_new)
            m_ref[g] = m_new
            l_ref[g] = alpha * l_ref[g] + jnp.sum(pt, axis=0, keepdims=True)
            acc_ref[g] = alpha * acc_ref[g] + jnp.dot(vtc, pt.astype(BF16),
                                                      preferred_element_type=F32)

    m_ref[...] = jnp.full(m_ref.shape, -jnp.inf, F32)
    l_ref[...] = jnp.zeros(l_ref.shape, F32)
    acc_ref[...] = jnp.zeros(acc_ref.shape, F32)
    for g in range(Q_PER_KV):
        scores(key_chunk(0), g, st_a, mx_a)

    def pair(j, carry):
        chunk(2 * j, st_a, mx_a, st_b, mx_b)
        chunk(2 * j + 1, st_b, mx_b, st_a, mx_a)
        return carry

    lax.fori_loop(0, nk // 2, pair, 0)
    _attn_finish(o_ref, l_ref, acc_ref)


def _attn_finish(o_ref, l_ref, acc_ref):
    for g in range(Q_PER_KV):
        out_t = acc_ref[g] * (1.0 / l_ref[g])
        o_ref[:, g * HEAD_DIM:(g + 1) * HEAD_DIM] = out_t.T.astype(o_ref.dtype)


def _attn_bounded_body(qt_ref, k_ref, vt_ref, o_ref, l_ref, acc_ref, *, tk, nk):
    l_ref[...] = jnp.zeros(l_ref.shape, F32)
    acc_ref[...] = jnp.zeros(acc_ref.shape, F32)

    def pair(j, carry):
        pending = None
        for u in range(2 * Q_PER_KV):
            c, g = 2 * j + u // Q_PER_KV, u % Q_PER_KV
            kc = k_ref[pl.ds(pl.multiple_of(c * tk, tk), tk), :]
            st = jnp.dot(kc, qt_ref[g * HEAD_DIM:(g + 1) * HEAD_DIM, :],
                         preferred_element_type=F32)
            if pending is not None:
                pc, pg, pt = pending
                acc_ref[pg] += jnp.dot(vt_ref[0, pc], pt, preferred_element_type=F32)
            pt = jnp.exp2(st)
            l_ref[g] += jnp.sum(pt, axis=0, keepdims=True)
            pending = (c, g, pt.astype(BF16))
        pc, pg, pt = pending
        acc_ref[pg] += jnp.dot(vt_ref[0, pc], pt, preferred_element_type=F32)
        return carry

    lax.fori_loop(0, nk // 2, pair, 0)
    _attn_finish(o_ref, l_ref, acc_ref)


def _attention(qt, k, vt, *, bounded, tq=512):
    s_len = k.shape[0]
    nk, tk = vt.shape[1], vt.shape[3]
    tq = min(tq, s_len)
    assert nk * tk == s_len and nk % 2 == 0 and s_len % tq == 0
    stat = pltpu.VMEM((Q_PER_KV, 1, tq), F32)
    acc = pltpu.VMEM((Q_PER_KV, HEAD_DIM, tq), F32)
    if bounded:
        body = functools.partial(_attn_bounded_body, tk=tk, nk=nk)
        scratch = [stat, acc]
    else:
        body = functools.partial(_attn_body, tk=tk, nk=nk)
        chunk = pltpu.VMEM((Q_PER_KV, tk, tq), F32)
        scratch = [chunk, chunk, stat, stat, stat, stat, acc]
    return pl.pallas_call(
        body,
        grid=(N_KV_HEADS, s_len // tq),
        in_specs=[pl.BlockSpec((GROUP_WIDTH, tq), lambda h, i: (h, i)),
                  pl.BlockSpec((s_len, HEAD_DIM), lambda h, i: (0, h)),
                  pl.BlockSpec((1, nk, HEAD_DIM, tk), lambda h, i: (h, 0, 0, 0))],
        out_specs=pl.BlockSpec((tq, GROUP_WIDTH), lambda h, i: (i, h)),
        out_shape=jax.ShapeDtypeStruct((s_len, D_MODEL), BF16),
        scratch_shapes=scratch,
        compiler_params=_params(("parallel", "arbitrary"), 48),
    )(qt, k, vt)


def _score_bound(q_gain, k_gain, q_scale):
    return 1.02 * HEAD_DIM * q_scale * jnp.max(jnp.abs(q_gain)) * jnp.max(jnp.abs(k_gain))


def _mem_body(qm_ref, mk_ref, mv_ref, y_hbm_ref, o_ref):
    del y_hbm_ref
    scale = MEM_HEAD_DIM ** -0.5
    for h in range(MEM_HEADS):
        cols = slice(h * MEM_HEAD_DIM, (h + 1) * MEM_HEAD_DIM)
        s = lax.dot_general(qm_ref[:, cols], mk_ref[:, cols], NT_DIMS,
                            preferred_element_type=F32) * scale
        p = jnp.exp(s - jnp.max(s, axis=-1, keepdims=True))
        p = p * (1.0 / jnp.sum(p, axis=-1, keepdims=True))
        o_ref[:, cols] = jnp.dot(p.astype(BF16), mv_ref[:, cols],
                                 preferred_element_type=F32).astype(o_ref.dtype)


def _mem_attention(qm_arr, qm_block, mem_kv, y, *, tm=1024):
    s_len = y.shape[0]
    tm = min(tm, s_len)
    out_block = MIXER_WIDTH // MEM_WIDTH
    return pl.pallas_call(
        _mem_body,
        grid=(s_len // tm,),
        in_specs=[pl.BlockSpec((tm, MEM_WIDTH), lambda i: (i, qm_block)),
                  pl.BlockSpec((N_MEM, MEM_WIDTH), lambda i: (0, 0)),
                  pl.BlockSpec((N_MEM, MEM_WIDTH), lambda i: (0, 1)),
                  pl.BlockSpec(memory_space=pl.ANY)],
        out_specs=pl.BlockSpec((tm, MEM_WIDTH), lambda i: (i, out_block)),
        out_shape=jax.ShapeDtypeStruct(y.shape, y.dtype),
        input_output_aliases={3: 0},
        compiler_params=_params(("parallel",), 32),
    )(qm_arr, mem_kv, mem_kv, y)


def _pool_body(up_ref, um_ref, un_ref, pw_ref, sc_ref, o_ref, *, tm, s_len):
    i = pl.program_id(0)
    g = pl.program_id(1)
    lo = jnp.left_shift(jnp.int32(1), g)
    hi = lo - 1
    u_ext = jnp.concatenate([up_ref[...], um_ref[...], un_ref[...]], axis=0)
    ext = tm + 2 * POOL_HALO
    t = lax.broadcasted_iota(jnp.int32, (tm, ext), 0)
    j = lax.broadcasted_iota(jnp.int32, (tm, ext), 1)
    d = j - POOL_HALO - t
    r = i * tm + j - POOL_HALO
    in_window = jnp.where(d >= -lo, 1.0, 0.0)
    in_window = jnp.where(d <= hi, in_window, 0.0)
    in_window = jnp.where(r >= 0, in_window, 0.0)
    in_window = jnp.where(r < s_len, in_window, 0.0)
    t_col = i * tm + lax.broadcasted_iota(jnp.int32, (tm, 1), 0)
    cnt = jnp.minimum(t_col + hi + 1, s_len) - jnp.maximum(t_col - lo, 0)
    cnt_f = cnt.astype(F32)
    band = jnp.where(d == 0, 1.0 - cnt_f, in_window).astype(BF16)
    pooled = jnp.dot(band, u_ext, preferred_element_type=F32) * (1.0 / cnt_f)
    y = jnp.dot(pooled.astype(BF16), pw_ref[0], preferred_element_type=F32) * sc_ref[...]
    o_ref[...] = y.astype(o_ref.dtype)


def _pool_mix(proj, pool_w, pool_scale, *, tm=512):
    s_len = proj.shape[0]
    tm = min(tm, s_len)
    hb = tm // POOL_HALO
    last = s_len // POOL_HALO - 1
    body = functools.partial(_pool_body, tm=tm, s_len=s_len)
    return pl.pallas_call(
        body,
        grid=(s_len // tm, 4),
        in_specs=[pl.BlockSpec((POOL_HALO, POOL_GROUP), lambda i, g: (jnp.maximum(i * hb - 1, 0), g)),
                  pl.BlockSpec((tm, POOL_GROUP), lambda i, g: (i, g)),
                  pl.BlockSpec((POOL_HALO, POOL_GROUP), lambda i, g: (jnp.minimum((i + 1) * hb, last), g)),
                  pl.BlockSpec((1, POOL_GROUP, POOL_GROUP), lambda i, g: (g, 0, 0)),
                  pl.BlockSpec((1, POOL_GROUP), lambda i, g: (0, g))],
        out_specs=pl.BlockSpec((tm, POOL_GROUP), lambda i, g: (i, g)),
        out_shape=jax.ShapeDtypeStruct((s_len, D_MODEL), BF16),
        compiler_params=_params(("parallel", "arbitrary"), 32),
    )(proj, proj, proj, pool_w, pool_scale)


def _deepnorm(h, y, gamma, beta):
    z = ALPHA * h + y
    mu = jnp.mean(z, axis=-1, keepdims=True)
    zc = z - mu
    var = jnp.mean(zc * zc, axis=-1, keepdims=True)
    return zc * lax.rsqrt(var + LN_EPS) * gamma + beta


def _deepnorm_rows(h_ref, g_ref, b_ref, o_ref, ob_ref):
    def rows_step(r, carry):
        rows = pl.ds(pl.multiple_of(r * LN_ROWS, LN_ROWS), LN_ROWS)
        out = _deepnorm(h_ref[rows, :], o_ref[rows, :], g_ref[...], b_ref[...])
        o_ref[rows, :] = out
        ob_ref[rows, :] = out.astype(ob_ref.dtype)
        return carry

    lax.fori_loop(0, o_ref.shape[0] // LN_ROWS, rows_step, 0)


def _ln_body(y_ref, h_ref, g_ref, b_ref, o_ref, ob_ref):
    out = _deepnorm(h_ref[...], y_ref[...], g_ref[...], b_ref[...])
    o_ref[...] = out
    ob_ref[...] = out.astype(ob_ref.dtype)


def _residual_ln(y, h, gamma, beta, *, tm=128):
    m, d = h.shape
    tm = min(tm, m)
    row = pl.BlockSpec((tm, d), lambda i: (i, 0))
    vec = pl.BlockSpec((1, d), lambda i: (0, 0))
    return pl.pallas_call(
        _ln_body,
        grid=(m // tm,),
        in_specs=[row, row, vec, vec],
        out_specs=[row, row],
        out_shape=[jax.ShapeDtypeStruct((m, d), F32), jax.ShapeDtypeStruct((m, d), BF16)],
        compiler_params=_params(("parallel",), 32),
    )(y, h, gamma, beta)


def _mlp_body(h_ref, hb_ref, w1_ref, w2_ref, g_ref, b_ref, o_ref, ob_ref, *, nf):
    f = pl.program_id(1)

    @pl.when(f == 0)
    def _():
        o_ref[...] = jnp.zeros_like(o_ref)

    a = jnp.dot(hb_ref[...], w1_ref[...], preferred_element_type=F32)
    a = jnp.square(jnp.maximum(a, 0.0)).astype(BF16)
    for n in range(0, o_ref.shape[1], MLP_OUT_CHUNK):
        cols = slice(n, n + MLP_OUT_CHUNK)
        o_ref[:, cols] += jnp.dot(a, w2_ref[:, cols], preferred_element_type=F32)

    @pl.when(f == nf - 1)
    def _():
        _deepnorm_rows(h_ref, g_ref, b_ref, o_ref, ob_ref)


def _mlp_ln(h, hb, w1, w2, gamma, beta, *, tm=512, tf=512):
    m, d = h.shape
    dff = w1.shape[1]
    tm = min(tm, m)
    body = functools.partial(_mlp_body, nf=dff // tf)
    row = lambda i, f: (i, 0)
    return pl.pallas_call(
        body,
        grid=(m // tm, dff // tf),
        in_specs=[pl.BlockSpec((tm, d), row, pipeline_mode=pl.Buffered(1)),
                  pl.BlockSpec((tm, d), row, pipeline_mode=pl.Buffered(1)),
                  pl.BlockSpec((d, tf), lambda i, f: (0, f)),
                  pl.BlockSpec((tf, d), lambda i, f: (f, 0)),
                  pl.BlockSpec((1, d), lambda i, f: (0, 0)),
                  pl.BlockSpec((1, d), lambda i, f: (0, 0))],
        out_specs=[pl.BlockSpec((tm, d), row), pl.BlockSpec((tm, d), row)],
        out_shape=[jax.ShapeDtypeStruct((m, d), F32), jax.ShapeDtypeStruct((m, d), BF16)],
        compiler_params=_params(("parallel", "arbitrary"), 56),
    )(h, hb, w1, w2, gamma, beta)


def _rope_tables(s_len):
    rows = s_len // GRID_W
    inv_freq = ROPE_THETA ** (-jnp.arange(N_FREQ, dtype=F32) / N_FREQ)
    row_ang = jnp.arange(rows, dtype=F32)[:, None, None] * inv_freq
    col_ang = jnp.arange(GRID_W, dtype=F32)[None, :, None] * inv_freq
    ang = jnp.stack([jnp.broadcast_to(row_ang, (rows, GRID_W, N_FREQ)),
                     jnp.broadcast_to(col_ang, (rows, GRID_W, N_FREQ))], axis=2).reshape(s_len, 2, 1, N_FREQ)
    cos_t = jnp.broadcast_to(jnp.cos(ang), (s_len, 2, 2, N_FREQ)).reshape(s_len, HEAD_DIM)
    sign = jnp.array([-1.0, 1.0], F32).reshape(1, 1, 2, 1)
    sin_t = (jnp.sin(ang) * sign).reshape(s_len, HEAD_DIM)
    return cos_t, sin_t


def _vec(p):
    return p.reshape(1, -1).astype(F32)


def kernel(x, mem, w_mem_kv, l0_w_in, l0_q_gain, l0_k_gain, l0_w_out, l0_ln1_g, l0_ln1_b, l0_w_ff1, l0_w_ff2, l0_ln2_g, l0_ln2_b, l1_w_in, l1_pool_w, l1_pool_scale, l1_w_out, l1_ln1_g, l1_ln1_b, l1_w_ff1, l1_w_ff2, l1_ln2_g, l1_ln2_b, l2_w_in, l2_q_gain, l2_k_gain, l2_w_out, l2_ln1_g, l2_ln1_b, l2_w_ff1, l2_w_ff2, l2_ln2_g, l2_ln2_b, l3_w_in, l3_pool_w, l3_pool_scale, l3_w_out, l3_ln1_g, l3_ln1_b, l3_w_ff1, l3_w_ff2, l3_ln2_g, l3_ln2_b):
    layers = [
        (l0_w_in, l0_q_gain, l0_k_gain, l0_w_out, l0_ln1_g, l0_ln1_b, l0_w_ff1, l0_w_ff2, l0_ln2_g, l0_ln2_b),
        (l1_w_in, l1_pool_w, l1_pool_scale, l1_w_out, l1_ln1_g, l1_ln1_b, l1_w_ff1, l1_w_ff2, l1_ln2_g, l1_ln2_b),
        (l2_w_in, l2_q_gain, l2_k_gain, l2_w_out, l2_ln1_g, l2_ln1_b, l2_w_ff1, l2_w_ff2, l2_ln2_g, l2_ln2_b),
        (l3_w_in, l3_pool_w, l3_pool_scale, l3_w_out, l3_ln1_g, l3_ln1_b, l3_w_ff1, l3_w_ff2, l3_ln2_g, l3_ln2_b),
    ]
    b, s_len, d = x.shape
    assert b == 1 and d == D_MODEL

    mem_kv = _matmul(mem[0].astype(BF16), w_mem_kv.astype(BF16),
                     col0=0, ncols=2 * MEM_WIDTH, tn=MEM_WIDTH, out_dtype=BF16)
    cos_t, sin_t = _rope_tables(s_len)
    q_scale = HEAD_DIM ** -0.5 * math.log2(math.e)
    cos_q, sin_q = cos_t * q_scale, sin_t * q_scale

    h = x[0]
    hb = h.astype(BF16)
    for i in range(DEPTH):
        w_in, pa, pb, w_out, ln1_g, ln1_b, w_ff1, w_ff2, ln2_g, ln2_b = layers[i]
        w_in = w_in.astype(BF16)
        if i % 2 == 0:
            qt = _qk_proj(hb, w_in, _vec(pa), cos_q, sin_q, col0=0, ncols=MIXER_WIDTH, tn=512,
                          transpose_out=True)
            k = _qk_proj(hb, w_in, _vec(pb), cos_t, sin_t, col0=MIXER_WIDTH, ncols=KV_WIDTH, tn=KV_WIDTH)
            vt = _v_proj(hb, w_in, col0=MIXER_WIDTH + KV_WIDTH, tk=min(ATTN_KEY_CHUNK, s_len // 2))
            qm = _matmul(hb, w_in, col0=MIXER_WIDTH + 2 * KV_WIDTH, ncols=MEM_WIDTH, tn=512, out_dtype=BF16)
            y = lax.cond(_score_bound(pa, pb, q_scale) <= SCORE_BOUND_LIMIT,
                         functools.partial(_attention, bounded=True),
                         functools.partial(_attention, bounded=False),
                         qt, k, vt)
            y = _mem_attention(qm, 0, mem_kv, y)
        else:
            proj = _matmul(hb, w_in, col0=0, ncols=D_MODEL, tn=1024, out_dtype=BF16)
            y = _pool_mix(proj, pa.astype(BF16), _vec(pb))
            y = _mem_attention(proj, MIXER_WIDTH // MEM_WIDTH, mem_kv, y)
        y = _matmul(y, w_out.astype(BF16), col0=0, ncols=D_MODEL, tn=1024, out_dtype=F32)
        h, hb = _residual_ln(y, h, _vec(ln1_g), _vec(ln1_b))
        h, hb = _mlp_ln(h, hb, w_ff1.astype(BF16), w_ff2.astype(BF16), _vec(ln2_g), _vec(ln2_b))
    return h[None]
```

```python
import functools
import math

import jax
import jax.numpy as jnp
from jax import lax
from jax.experimental import pallas as pl
from jax.experimental.pallas import tpu as pltpu

D_MODEL = 4096
DEPTH = 4
GRID_W = 64
N_MEM = 256
MEM_HEADS = 4
MEM_HEAD_DIM = 256
MEM_WIDTH = MEM_HEADS * MEM_HEAD_DIM
MIXER_WIDTH = D_MODEL - MEM_WIDTH
HEAD_DIM = 128
N_KV_HEADS = 6
Q_PER_KV = 4
GROUP_WIDTH = Q_PER_KV * HEAD_DIM
KV_WIDTH = N_KV_HEADS * HEAD_DIM
ATTN_KEY_CHUNK = 512
SCORE_BOUND_LIMIT = 64.0
ROPE_THETA = 10000.0
N_FREQ = HEAD_DIM // 4
POOL_GROUP = MIXER_WIDTH // 4
POOL_HALO = 16
D_FF = 4 * D_MODEL
MLP_OUT_CHUNK = 512
LN_ROWS = 64
ALPHA = (2 * DEPTH) ** 0.25
LN_EPS = 1e-5
RMS_EPS = 1e-6

MIB = 1024 * 1024
BF16 = jnp.bfloat16
F32 = jnp.float32
NT_DIMS = (((1,), (1,)), ((), ()))


def _params(semantics, vmem_mib):
    return pltpu.CompilerParams(dimension_semantics=semantics, vmem_limit_bytes=vmem_mib * MIB)


def _mm_body(a_ref, b_ref, o_ref):
    o_ref[...] = jnp.dot(a_ref[...], b_ref[...], preferred_element_type=F32).astype(o_ref.dtype)


def _matmul(a, b, *, col0, ncols, tn, out_dtype, tm=1024):
    m, k = a.shape
    tm = min(tm, m)
    nb0 = col0 // tn
    assert col0 % tn == 0 and ncols % tn == 0 and m % tm == 0
    return pl.pallas_call(
        _mm_body,
        grid=(m // tm, ncols // tn),
        in_specs=[pl.BlockSpec((tm, k), lambda i, j: (i, 0)),
                  pl.BlockSpec((k, tn), lambda i, j: (0, j + nb0))],
        out_specs=pl.BlockSpec((tm, tn), lambda i, j: (i, j)),
        out_shape=jax.ShapeDtypeStruct((m, ncols), out_dtype),
        compiler_params=_params(("parallel", "arbitrary"), 48),
    )(a, b)


def _v_body(a_ref, b_ref, o_ref, *, tk):
    acc_t = jnp.dot(a_ref[...], b_ref[...], preferred_element_type=F32).T
    for h in range(N_KV_HEADS):
        for c in range(o_ref.shape[1]):
            o_ref[h, c] = acc_t[h * HEAD_DIM:(h + 1) * HEAD_DIM,
                                c * tk:(c + 1) * tk].astype(o_ref.dtype)


def _v_proj(a, b, *, col0, tk, tm=1024):
    m, k = a.shape
    tm = min(tm, m)
    assert col0 % KV_WIDTH == 0 and m % tm == 0 and tm % tk == 0
    nb0 = col0 // KV_WIDTH
    return pl.pallas_call(
        functools.partial(_v_body, tk=tk),
        grid=(m // tm,),
        in_specs=[pl.BlockSpec((tm, k), lambda i: (i, 0)),
                  pl.BlockSpec((k, KV_WIDTH), lambda i: (0, nb0))],
        out_specs=pl.BlockSpec((N_KV_HEADS, tm // tk, HEAD_DIM, tk), lambda i: (0, i, 0, 0)),
        out_shape=jax.ShapeDtypeStruct((N_KV_HEADS, m // tk, HEAD_DIM, tk), BF16),
        compiler_params=_params(("parallel",), 48),
    )(a, b)


def _qk_body(a_ref, b_ref, gain_ref, cos_ref, sin_ref, o_ref, raw_a, raw_b, *, transpose_out):
    s = pl.program_id(0)

    def epilogue(raw_ref):
        tm, tn = raw_ref.shape
        lane = lax.broadcasted_iota(jnp.int32, (tm, HEAD_DIM), 1)
        first_half = (lane & (N_FREQ)) == 0
        gain = gain_ref[...]
        cos = cos_ref[...]
        sin = sin_ref[...]
        for j in range(tn // HEAD_DIM):
            x = raw_ref[:, j * HEAD_DIM:(j + 1) * HEAD_DIM]
            ms = jnp.mean(x * x, axis=-1, keepdims=True)
            xn = x * lax.rsqrt(ms + RMS_EPS) * gain
            partner = jnp.where(first_half,
                                pltpu.roll(xn, HEAD_DIM - N_FREQ, 1),
                                pltpu.roll(xn, N_FREQ, 1))
            out = xn * cos + partner * sin
            if transpose_out:
                o_ref[j * HEAD_DIM:(j + 1) * HEAD_DIM, :] = out.T.astype(o_ref.dtype)
            else:
                o_ref[:, j * HEAD_DIM:(j + 1) * HEAD_DIM] = out.astype(o_ref.dtype)

    def step(raw_dst, raw_src):
        raw_dst[...] = jnp.dot(a_ref[...], b_ref[...], preferred_element_type=F32)
        epilogue(raw_src)

    @pl.when(s == 0)
    def _():
        raw_b[...] = jnp.zeros_like(raw_b)

    @pl.when(s % 2 == 0)
    def _():
        step(raw_a, raw_b)

    @pl.when(s % 2 == 1)
    def _():
        step(raw_b, raw_a)


def _qk_proj(a, b, gain, cos_t, sin_t, *, col0, ncols, tn, tm=1024, transpose_out=False):
    m, k = a.shape
    tm = min(tm, m)
    nb0 = col0 // tn
    assert col0 % tn == 0 and ncols % tn == 0 and m % tm == 0
    nj = ncols // tn
    tiles = (m // tm) * nj

    def mm_tile(s):
        return jnp.minimum(s, tiles - 1)

    def ep_tile(s):
        return jnp.maximum(s - 1, 0)

    if transpose_out:
        out_spec = pl.BlockSpec((tn, tm), lambda s: (ep_tile(s) % nj, ep_tile(s) // nj))
        out_shape = jax.ShapeDtypeStruct((ncols, m), BF16)
    else:
        out_spec = pl.BlockSpec((tm, tn), lambda s: (ep_tile(s) // nj, ep_tile(s) % nj))
        out_shape = jax.ShapeDtypeStruct((m, ncols), BF16)
    table = pl.BlockSpec((tm, HEAD_DIM), lambda s: (ep_tile(s) // nj, 0))
    return pl.pallas_call(
        functools.partial(_qk_body, transpose_out=transpose_out),
        grid=(tiles + 1,),
        in_specs=[pl.BlockSpec((tm, k), lambda s: (mm_tile(s) // nj, 0)),
                  pl.BlockSpec((k, tn), lambda s: (0, mm_tile(s) % nj + nb0)),
                  pl.BlockSpec((1, HEAD_DIM), lambda s: (0, 0)),
                  table, table],
        out_specs=out_spec,
        out_shape=out_shape,
        scratch_shapes=[pltpu.VMEM((tm, tn), F32), pltpu.VMEM((tm, tn), F32)],
        compiler_params=_params(("arbitrary",), 56),
    )(a, b, gain, cos_t, sin_t)


def _attn_body(qt_ref, k_ref, vt_ref, o_ref, st_a, st_b, mx_a, mx_b, m_ref, l_ref, acc_ref, *, tk, nk):
    def scores(kc, g, dst, mx_dst):
        st = jnp.dot(kc, qt_ref[g * HEAD_DIM:(g + 1) * HEAD_DIM, :],
                     preferred_element_type=F32)
        dst[g] = st
        mx_dst[g] = jnp.max(st, axis=0, keepdims=True)

    def key_chunk(c):
        return k_ref[pl.ds(pl.multiple_of(c * tk, tk), tk), :]

    def chunk(c, src, mx_src, dst, mx_dst):
        kc_next = key_chunk(jnp.minimum(c + 1, nk - 1))
        vtc = vt_ref[0, c]
        for g in range(Q_PER_KV):
            scores(kc_next, g, dst, mx_dst)
            m = m_ref[g]
            m_new = jnp.maximum(m, mx_src[g])
            alpha = jnp.exp2(m - m_new)
            pt = jnp.exp2(src[g] - m_new)
            m_ref[g] = m_new
            l_ref[g] = alpha * l_ref[g] + jnp.sum(pt, axis=0, keepdims=True)
            acc_ref[g] = alpha * acc_ref[g] + jnp.dot(vtc, pt.astype(BF16),
                                                      preferred_element_type=F32)

    m_ref[...] = jnp.full(m_ref.shape, -jnp.inf, F32)
    l_ref[...] = jnp.zeros(l_ref.shape, F32)
    acc_ref[...] = jnp.zeros(acc_ref.shape, F32)
    for g in range(Q_PER_KV):
        scores(key_chunk(0), g, st_a, mx_a)

    def pair(j, carry):
        chunk(2 * j, st_a, mx_a, st_b, mx_b)
        chunk(2 * j + 1, st_b, mx_b, st_a, mx_a)
        return carry

    lax.fori_loop(0, nk // 2, pair, 0)
    _attn_finish(o_ref, l_ref, acc_ref)


def _attn_finish(o_ref, l_ref, acc_ref):
    for g in range(Q_PER_KV):
        out_t = acc_ref[g] * (1.0 / l_ref[g])
        o_ref[:, g * HEAD_DIM:(g + 1) * HEAD_DIM] = out_t.T.astype(o_ref.dtype)


def _attn_bounded_body(qt_ref, k_ref, vt_ref, o_ref, l_ref, acc_ref, *, tk, nk):
    l_ref[...] = jnp.zeros(l_ref.shape, F32)
    acc_ref[...] = jnp.zeros(acc_ref.shape, F32)

    def pair(j, carry):
        pending = None
        for u in range(2 * Q_PER_KV):
            c, g = 2 * j + u // Q_PER_KV, u % Q_PER_KV
            kc = k_ref[pl.ds(pl.multiple_of(c * tk, tk), tk), :]
            st = jnp.dot(kc, qt_ref[g * HEAD_DIM:(g + 1) * HEAD_DIM, :],
                         preferred_element_type=F32)
            if pending is not None:
                pc, pg, pt = pending
                acc_ref[pg] += jnp.dot(vt_ref[0, pc], pt, preferred_element_type=F32)
            pt = jnp.exp2(st)
            l_ref[g] += jnp.sum(pt, axis=0, keepdims=True)
            pending = (c, g, pt.astype(BF16))
        pc, pg, pt = pending
        acc_ref[pg] += jnp.dot(vt_ref[0, pc], pt, preferred_element_type=F32)
        return carry

    lax.fori_loop(0, nk // 2, pair, 0)
    _attn_finish(o_ref, l_ref, acc_ref)


def _attention(qt, k, vt, *, bounded, tq=512):
    s_len = k.shape[0]
    nk, tk = vt.shape[1], vt.shape[3]
    tq = min(tq, s_len)
    assert nk * tk == s_len and nk % 2 == 0 and s_len % tq == 0
    stat = pltpu.VMEM((Q_PER_KV, 1, tq), F32)
    acc = pltpu.VMEM((Q_PER_KV, HEAD_DIM, tq), F32)
    if bounded:
        body = functools.partial(_attn_bounded_body, tk=tk, nk=nk)
        scratch = [stat, acc]
    else:
        body = functools.partial(_attn_body, tk=tk, nk=nk)
        chunk = pltpu.VMEM((Q_PER_KV, tk, tq), F32)
        scratch = [chunk, chunk, stat, stat, stat, stat, acc]
    return pl.pallas_call(
        body,
        grid=(N_KV_HEADS, s_len // tq),
        in_specs=[pl.BlockSpec((GROUP_WIDTH, tq), lambda h, i: (h, i)),
                  pl.BlockSpec((s_len, HEAD_DIM), lambda h, i: (0, h)),
                  pl.BlockSpec((1, nk, HEAD_DIM, tk), lambda h, i: (h, 0, 0, 0))],
        out_specs=pl.BlockSpec((tq, GROUP_WIDTH), lambda h, i: (i, h)),
        out_shape=jax.ShapeDtypeStruct((s_len, D_MODEL), BF16),
        scratch_shapes=scratch,
        compiler_params=_params(("parallel", "arbitrary"), 48),
    )(qt, k, vt)


def _score_bound(q_gain, k_gain, q_scale):
    return 1.02 * HEAD_DIM * q_scale * jnp.max(jnp.abs(q_gain)) * jnp.max(jnp.abs(k_gain))


def _mem_body(qm_ref, mk_ref, mv_ref, y_hbm_ref, o_ref):
    del y_hbm_ref
    scale = MEM_HEAD_DIM ** -0.5
    for h in range(MEM_HEADS):
        cols = slice(h * MEM_HEAD_DIM, (h + 1) * MEM_HEAD_DIM)
        s = lax.dot_general(qm_ref[:, cols], mk_ref[:, cols], NT_DIMS,
                            preferred_element_type=F32) * scale
        p = jnp.exp(s - jnp.max(s, axis=-1, keepdims=True))
        p = p * (1.0 / jnp.sum(p, axis=-1, keepdims=True))
        o_ref[:, cols] = jnp.dot(p.astype(BF16), mv_ref[:, cols],
                                 preferred_element_type=F32).astype(o_ref.dtype)


def _mem_attention(qm_arr, qm_block, mem_kv, y, *, tm=1024):
    s_len = y.shape[0]
    tm = min(tm, s_len)
    out_block = MIXER_WIDTH // MEM_WIDTH
    return pl.pallas_call(
        _mem_body,
        grid=(s_len // tm,),
        in_specs=[pl.BlockSpec((tm, MEM_WIDTH), lambda i: (i, qm_block)),
                  pl.BlockSpec((N_MEM, MEM_WIDTH), lambda i: (0, 0)),
                  pl.BlockSpec((N_MEM, MEM_WIDTH), lambda i: (0, 1)),
                  pl.BlockSpec(memory_space=pl.ANY)],
        out_specs=pl.BlockSpec((tm, MEM_WIDTH), lambda i: (i, out_block)),
        out_shape=jax.ShapeDtypeStruct(y.shape, y.dtype),
        input_output_aliases={3: 0},
        compiler_params=_params(("parallel",), 32),
    )(qm_arr, mem_kv, mem_kv, y)


def _pool_body(up_ref, um_ref, un_ref, pw_ref, sc_ref, o_ref, *, tm, s_len):
    i = pl.program_id(0)
    g = pl.program_id(1)
    lo = jnp.left_shift(jnp.int32(1), g)
    hi = lo - 1
    u_ext = jnp.concatenate([up_ref[...], um_ref[...], un_ref[...]], axis=0)
    ext = tm + 2 * POOL_HALO
    t = lax.broadcasted_iota(jnp.int32, (tm, ext), 0)
    j = lax.broadcasted_iota(jnp.int32, (tm, ext), 1)
    d = j - POOL_HALO - t
    r = i * tm + j - POOL_HALO
    in_window = jnp.where(d >= -lo, 1.0, 0.0)
    in_window = jnp.where(d <= hi, in_window, 0.0)
    in_window = jnp.where(r >= 0, in_window, 0.0)
    in_window = jnp.where(r < s_len, in_window, 0.0)
    t_col = i * tm + lax.broadcasted_iota(jnp.int32, (tm, 1), 0)
    cnt = jnp.minimum(t_col + hi + 1, s_len) - jnp.maximum(t_col - lo, 0)
    cnt_f = cnt.astype(F32)
    band = jnp.where(d == 0, 1.0 - cnt_f, in_window).astype(BF16)
    pooled = jnp.dot(band, u_ext, preferred_element_type=F32) * (1.0 / cnt_f)
    y = jnp.dot(pooled.astype(BF16), pw_ref[0], preferred_element_type=F32) * sc_ref[...]
    o_ref[...] = y.astype(o_ref.dtype)


def _pool_mix(proj, pool_w, pool_scale, *, tm=512):
    s_len = proj.shape[0]
    tm = min(tm, s_len)
    hb = tm // POOL_HALO
    last = s_len // POOL_HALO - 1
    body = functools.partial(_pool_body, tm=tm, s_len=s_len)
    return pl.pallas_call(
        body,
        grid=(s_len // tm, 4),
        in_specs=[pl.BlockSpec((POOL_HALO, POOL_GROUP), lambda i, g: (jnp.maximum(i * hb - 1, 0), g)),
                  pl.BlockSpec((tm, POOL_GROUP), lambda i, g: (i, g)),
                  pl.BlockSpec((POOL_HALO, POOL_GROUP), lambda i, g: (jnp.minimum((i + 1) * hb, last), g)),
                  pl.BlockSpec((1, POOL_GROUP, POOL_GROUP), lambda i, g: (g, 0, 0)),
                  pl.BlockSpec((1, POOL_GROUP), lambda i, g: (0, g))],
        out_specs=pl.BlockSpec((tm, POOL_GROUP), lambda i, g: (i, g)),
        out_shape=jax.ShapeDtypeStruct((s_len, D_MODEL), BF16),
        compiler_params=_params(("parallel", "arbitrary"), 32),
    )(proj, proj, proj, pool_w, pool_scale)


def _deepnorm(h, y, gamma, beta):
    z = ALPHA * h + y
    mu = jnp.mean(z, axis=-1, keepdims=True)
    zc = z - mu
    var = jnp.mean(zc * zc, axis=-1, keepdims=True)
    return zc * lax.rsqrt(var + LN_EPS) * gamma + beta


def _deepnorm_rows(hb_ref, g_ref, b_ref, y_ref, o_ref):
    def rows_step(r, carry):
        rows = pl.ds(pl.multiple_of(r * LN_ROWS, LN_ROWS), LN_ROWS)
        out = _deepnorm(hb_ref[rows, :].astype(F32), y_ref[rows, :], g_ref[...], b_ref[...])
        o_ref[rows, :] = out.astype(o_ref.dtype)
        return carry

    lax.fori_loop(0, o_ref.shape[0] // LN_ROWS, rows_step, 0)


def _ln_body(y_ref, hb_ref, g_ref, b_ref, o_ref):
    out = _deepnorm(hb_ref[...].astype(F32), y_ref[...], g_ref[...], b_ref[...])
    o_ref[...] = out.astype(o_ref.dtype)


def _residual_ln(y, hb, gamma, beta, *, tm=256):
    m, d = hb.shape
    tm = min(tm, m)
    row = pl.BlockSpec((tm, d), lambda i: (i, 0))
    vec = pl.BlockSpec((1, d), lambda i: (0, 0))
    return pl.pallas_call(
        _ln_body,
        grid=(m // tm,),
        in_specs=[row, row, vec, vec],
        out_specs=row,
        out_shape=jax.ShapeDtypeStruct((m, d), BF16),
        compiler_params=_params(("parallel",), 32),
    )(y, hb, gamma, beta)


def _mlp_body(hb_ref, w1_ref, w2_ref, g_ref, b_ref, o_ref, *scratch, nf):
    acc_ref = scratch[0] if scratch else o_ref
    f = pl.program_id(1)

    @pl.when(f == 0)
    def _():
        acc_ref[...] = jnp.zeros_like(acc_ref)

    a = jnp.dot(hb_ref[...], w1_ref[...], preferred_element_type=F32)
    a = jnp.square(jnp.maximum(a, 0.0)).astype(BF16)
    for n in range(0, acc_ref.shape[1], MLP_OUT_CHUNK):
        cols = slice(n, n + MLP_OUT_CHUNK)
        acc_ref[:, cols] += jnp.dot(a, w2_ref[:, cols], preferred_element_type=F32)

    @pl.when(f == nf - 1)
    def _():
        _deepnorm_rows(hb_ref, g_ref, b_ref, acc_ref, o_ref)


def _mlp_ln(hb, w1, w2, gamma, beta, *, out_dtype, tm=1024, tf=512):
    m, d = hb.shape
    dff = w1.shape[1]
    tm = min(tm, m)
    body = functools.partial(_mlp_body, nf=dff // tf)
    row = lambda i, f: (i, 0)
    scratch = [] if out_dtype == F32 else [pltpu.VMEM((tm, d), F32)]
    return pl.pallas_call(
        body,
        grid=(m // tm, dff // tf),
        in_specs=[pl.BlockSpec((tm, d), row, pipeline_mode=pl.Buffered(1)),
                  pl.BlockSpec((d, tf), lambda i, f: (0, f)),
                  pl.BlockSpec((tf, d), lambda i, f: (f, 0)),
                  pl.BlockSpec((1, d), lambda i, f: (0, 0)),
                  pl.BlockSpec((1, d), lambda i, f: (0, 0))],
        out_specs=pl.BlockSpec((tm, d), row, pipeline_mode=pl.Buffered(1)),
        out_shape=jax.ShapeDtypeStruct((m, d), out_dtype),
        scratch_shapes=scratch,
        compiler_params=_params(("parallel", "arbitrary"), 60),
    )(hb, w1, w2, gamma, beta)


def _rope_tables(s_len):
    rows = s_len // GRID_W
    inv_freq = ROPE_THETA ** (-jnp.arange(N_FREQ, dtype=F32) / N_FREQ)
    row_ang = jnp.arange(rows, dtype=F32)[:, None, None] * inv_freq
    col_ang = jnp.arange(GRID_W, dtype=F32)[None, :, None] * inv_freq
    ang = jnp.stack([jnp.broadcast_to(row_ang, (rows, GRID_W, N_FREQ)),
                     jnp.broadcast_to(col_ang, (rows, GRID_W, N_FREQ))], axis=2).reshape(s_len, 2, 1, N_FREQ)
    cos_t = jnp.broadcast_to(jnp.cos(ang), (s_len, 2, 2, N_FREQ)).reshape(s_len, HEAD_DIM)
    sign = jnp.array([-1.0, 1.0], F32).reshape(1, 1, 2, 1)
    sin_t = (jnp.sin(ang) * sign).reshape(s_len, HEAD_DIM)
    return cos_t, sin_t


def _vec(p):
    return p.reshape(1, -1).astype(F32)


def kernel(x, mem, w_mem_kv, l0_w_in, l0_q_gain, l0_k_gain, l0_w_out, l0_ln1_g, l0_ln1_b, l0_w_ff1, l0_w_ff2, l0_ln2_g, l0_ln2_b, l1_w_in, l1_pool_w, l1_pool_scale, l1_w_out, l1_ln1_g, l1_ln1_b, l1_w_ff1, l1_w_ff2, l1_ln2_g, l1_ln2_b, l2_w_in, l2_q_gain, l2_k_gain, l2_w_out, l2_ln1_g, l2_ln1_b, l2_w_ff1, l2_w_ff2, l2_ln2_g, l2_ln2_b, l3_w_in, l3_pool_w, l3_pool_scale, l3_w_out, l3_ln1_g, l3_ln1_b, l3_w_ff1, l3_w_ff2, l3_ln2_g, l3_ln2_b):
    layers = [
        (l0_w_in, l0_q_gain, l0_k_gain, l0_w_out, l0_ln1_g, l0_ln1_b, l0_w_ff1, l0_w_ff2, l0_ln2_g, l0_ln2_b),
        (l1_w_in, l1_pool_w, l1_pool_scale, l1_w_out, l1_ln1_g, l1_ln1_b, l1_w_ff1, l1_w_ff2, l1_ln2_g, l1_ln2_b),
        (l2_w_in, l2_q_gain, l2_k_gain, l2_w_out, l2_ln1_g, l2_ln1_b, l2_w_ff1, l2_w_ff2, l2_ln2_g, l2_ln2_b),
        (l3_w_in, l3_pool_w, l3_pool_scale, l3_w_out, l3_ln1_g, l3_ln1_b, l3_w_ff1, l3_w_ff2, l3_ln2_g, l3_ln2_b),
    ]
    b, s_len, d = x.shape
    assert b == 1 and d == D_MODEL

    mem_kv = _matmul(mem[0].astype(BF16), w_mem_kv.astype(BF16),
                     col0=0, ncols=2 * MEM_WIDTH, tn=MEM_WIDTH, out_dtype=BF16)
    cos_t, sin_t = _rope_tables(s_len)
    q_scale = HEAD_DIM ** -0.5 * math.log2(math.e)
    cos_q, sin_q = cos_t * q_scale, sin_t * q_scale

    hb = x[0].astype(BF16)
    for i in range(DEPTH):
        w_in, pa, pb, w_out, ln1_g, ln1_b, w_ff1, w_ff2, ln2_g, ln2_b = layers[i]
        w_in = w_in.astype(BF16)
        if i % 2 == 0:
            qt = _qk_proj(hb, w_in, _vec(pa), cos_q, sin_q, col0=0, ncols=MIXER_WIDTH, tn=512,
                          transpose_out=True)
            k = _qk_proj(hb, w_in, _vec(pb), cos_t, sin_t, col0=MIXER_WIDTH, ncols=KV_WIDTH, tn=KV_WIDTH)
            vt = _v_proj(hb, w_in, col0=MIXER_WIDTH + KV_WIDTH, tk=min(ATTN_KEY_CHUNK, s_len // 2))
            qm = _matmul(hb, w_in, col0=MIXER_WIDTH + 2 * KV_WIDTH, ncols=MEM_WIDTH, tn=512, out_dtype=BF16)
            y = lax.cond(_score_bound(pa, pb, q_scale) <= SCORE_BOUND_LIMIT,
                         functools.partial(_attention, bounded=True),
                         functools.partial(_attention, bounded=False),
                         qt, k, vt)
            y = _mem_attention(qm, 0, mem_kv, y)
        else:
            proj = _matmul(hb, w_in, col0=0, ncols=D_MODEL, tn=1024, out_dtype=BF16)
            y = _pool_mix(proj, pa.astype(BF16), _vec(pb))
            y = _mem_attention(proj, MIXER_WIDTH // MEM_WIDTH, mem_kv, y)
        y = _matmul(y, w_out.astype(BF16), col0=0, ncols=D_MODEL, tn=1024, out_dtype=F32)
        hb = _residual_ln(y, hb, _vec(ln1_g), _vec(ln1_b))
        hb = _mlp_ln(hb, w_ff1.astype(BF16), w_ff2.astype(BF16), _vec(ln2_g), _vec(ln2_b),
                     out_dtype=F32 if i == DEPTH - 1 else BF16)
    return hb[None]
```

```python
import functools
import math

import jax
import jax.numpy as jnp
from jax import lax
from jax.experimental import pallas as pl
from jax.experimental.pallas import tpu as pltpu

D_MODEL = 4096
DEPTH = 4
GRID_W = 64
N_MEM = 256
MEM_HEADS = 4
MEM_HEAD_DIM = 256
MEM_WIDTH = MEM_HEADS * MEM_HEAD_DIM
MIXER_WIDTH = D_MODEL - MEM_WIDTH
HEAD_DIM = 128
N_KV_HEADS = 6
Q_PER_KV = 4
GROUP_WIDTH = Q_PER_KV * HEAD_DIM
KV_WIDTH = N_KV_HEADS * HEAD_DIM
ATTN_KEY_CHUNK = 512
SCORE_BOUND_LIMIT = 64.0
ROPE_THETA = 10000.0
N_FREQ = HEAD_DIM // 4
POOL_GROUP = MIXER_WIDTH // 4
POOL_HALO = 16
D_FF = 4 * D_MODEL
MLP_OUT_CHUNK = 512
LN_ROWS = 64
ALPHA = (2 * DEPTH) ** 0.25
LN_EPS = 1e-5
RMS_EPS = 1e-6

MIB = 1024 * 1024
BF16 = jnp.bfloat16
F32 = jnp.float32
NT_DIMS = (((1,), (1,)), ((), ()))


def _params(semantics, vmem_mib):
    return pltpu.CompilerParams(dimension_semantics=semantics, vmem_limit_bytes=vmem_mib * MIB)


def _mm_body(a_ref, b_ref, o_ref):
    o_ref[...] = jnp.dot(a_ref[...], b_ref[...], preferred_element_type=F32).astype(o_ref.dtype)


def _matmul(a, b, *, col0, ncols, tn, out_dtype, tm=1024):
    m, k = a.shape
    tm = min(tm, m)
    nb0 = col0 // tn
    assert col0 % tn == 0 and ncols % tn == 0 and m % tm == 0
    return pl.pallas_call(
        _mm_body,
        grid=(m // tm, ncols // tn),
        in_specs=[pl.BlockSpec((tm, k), lambda i, j: (i, 0)),
                  pl.BlockSpec((k, tn), lambda i, j: (0, j + nb0))],
        out_specs=pl.BlockSpec((tm, tn), lambda i, j: (i, j)),
        out_shape=jax.ShapeDtypeStruct((m, ncols), out_dtype),
        compiler_params=_params(("parallel", "arbitrary"), 48),
    )(a, b)


def _v_body(a_ref, b_ref, o_ref, *, tk):
    acc_t = jnp.dot(a_ref[...], b_ref[...], preferred_element_type=F32).T
    for h in range(N_KV_HEADS):
        for c in range(o_ref.shape[1]):
            o_ref[h, c] = acc_t[h * HEAD_DIM:(h + 1) * HEAD_DIM,
                                c * tk:(c + 1) * tk].astype(o_ref.dtype)


def _v_proj(a, b, *, col0, tk, tm=1024):
    m, k = a.shape
    tm = min(tm, m)
    assert col0 % KV_WIDTH == 0 and m % tm == 0 and tm % tk == 0
    nb0 = col0 // KV_WIDTH
    return pl.pallas_call(
        functools.partial(_v_body, tk=tk),
        grid=(m // tm,),
        in_specs=[pl.BlockSpec((tm, k), lambda i: (i, 0)),
                  pl.BlockSpec((k, KV_WIDTH), lambda i: (0, nb0))],
        out_specs=pl.BlockSpec((N_KV_HEADS, tm // tk, HEAD_DIM, tk), lambda i: (0, i, 0, 0)),
        out_shape=jax.ShapeDtypeStruct((N_KV_HEADS, m // tk, HEAD_DIM, tk), BF16),
        compiler_params=_params(("parallel",), 48),
    )(a, b)


def _qk_body(a_ref, b_ref, gain_ref, cos_ref, sin_ref, o_ref, raw_a, raw_b, *, transpose_out):
    s = pl.program_id(0)

    def epilogue(raw_ref):
        tm, tn = raw_ref.shape
        lane = lax.broadcasted_iota(jnp.int32, (tm, HEAD_DIM), 1)
        first_half = (lane & (N_FREQ)) == 0
        gain = gain_ref[...]
        cos = cos_ref[...]
        sin = sin_ref[...]
        for j in range(tn // HEAD_DIM):
            x = raw_ref[:, j * HEAD_DIM:(j + 1) * HEAD_DIM]
            ms = jnp.mean(x * x, axis=-1, keepdims=True)
            xn = x * lax.rsqrt(ms + RMS_EPS) * gain
            partner = jnp.where(first_half,
                                pltpu.roll(xn, HEAD_DIM - N_FREQ, 1),
                                pltpu.roll(xn, N_FREQ, 1))
            out = xn * cos + partner * sin
            if transpose_out:
                o_ref[j * HEAD_DIM:(j + 1) * HEAD_DIM, :] = out.T.astype(o_ref.dtype)
            else:
                o_ref[:, j * HEAD_DIM:(j + 1) * HEAD_DIM] = out.astype(o_ref.dtype)

    def step(raw_dst, raw_src):
        raw_dst[...] = jnp.dot(a_ref[...], b_ref[...], preferred_element_type=F32)
        epilogue(raw_src)

    @pl.when(s == 0)
    def _():
        raw_b[...] = jnp.zeros_like(raw_b)

    @pl.when(s % 2 == 0)
    def _():
        step(raw_a, raw_b)

    @pl.when(s % 2 == 1)
    def _():
        step(raw_b, raw_a)


def _qk_proj(a, b, gain, cos_t, sin_t, *, col0, ncols, tn, tm=1024, transpose_out=False):
    m, k = a.shape
    tm = min(tm, m)
    nb0 = col0 // tn
    assert col0 % tn == 0 and ncols % tn == 0 and m % tm == 0
    nj = ncols // tn
    tiles = (m // tm) * nj

    def mm_tile(s):
        return jnp.minimum(s, tiles - 1)

    def ep_tile(s):
        return jnp.maximum(s - 1, 0)

    if transpose_out:
        out_spec = pl.BlockSpec((tn, tm), lambda s: (ep_tile(s) % nj, ep_tile(s) // nj))
        out_shape = jax.ShapeDtypeStruct((ncols, m), BF16)
    else:
        out_spec = pl.BlockSpec((tm, tn), lambda s: (ep_tile(s) // nj, ep_tile(s) % nj))
        out_shape = jax.ShapeDtypeStruct((m, ncols), BF16)
    table = pl.BlockSpec((tm, HEAD_DIM), lambda s: (ep_tile(s) // nj, 0))
    return pl.pallas_call(
        functools.partial(_qk_body, transpose_out=transpose_out),
        grid=(tiles + 1,),
        in_specs=[pl.BlockSpec((tm, k), lambda s: (mm_tile(s) // nj, 0)),
                  pl.BlockSpec((k, tn), lambda s: (0, mm_tile(s) % nj + nb0)),
                  pl.BlockSpec((1, HEAD_DIM), lambda s: (0, 0)),
                  table, table],
        out_specs=out_spec,
        out_shape=out_shape,
        scratch_shapes=[pltpu.VMEM((tm, tn), F32), pltpu.VMEM((tm, tn), F32)],
        compiler_params=_params(("arbitrary",), 56),
    )(a, b, gain, cos_t, sin_t)


def _attn_body(qt_ref, k_ref, vt_ref, o_ref, st_a, st_b, mx_a, mx_b, m_ref, l_ref, acc_ref, *, tk, nk):
    def scores(kc, g, dst, mx_dst):
        st = jnp.dot(kc, qt_ref[g * HEAD_DIM:(g + 1) * HEAD_DIM, :],
                     preferred_element_type=F32)
        dst[g] = st
        mx_dst[g] = jnp.max(st, axis=0, keepdims=True)

    def key_chunk(c):
        return k_ref[pl.ds(pl.multiple_of(c * tk, tk), tk), :]

    def chunk(c, src, mx_src, dst, mx_dst):
        kc_next = key_chunk(jnp.minimum(c + 1, nk - 1))
        vtc = vt_ref[0, c]
        for g in range(Q_PER_KV):
            scores(kc_next, g, dst, mx_dst)
            m = m_ref[g]
            m_new = jnp.maximum(m, mx_src[g])
            alpha = jnp.exp2(m - m_new)
            pt = jnp.exp2(src[g] - m_new)
            m_ref[g] = m_new
            l_ref[g] = alpha * l_ref[g] + jnp.sum(pt, axis=0, keepdims=True)
            acc_ref[g] = alpha * acc_ref[g] + jnp.dot(vtc, pt.astype(BF16),
                                                      preferred_element_type=F32)

    m_ref[...] = jnp.full(m_ref.shape, -jnp.inf, F32)
    l_ref[...] = jnp.zeros(l_ref.shape, F32)
    acc_ref[...] = jnp.zeros(acc_ref.shape, F32)
    for g in range(Q_PER_KV):
        scores(key_chunk(0), g, st_a, mx_a)

    def pair(j, carry):
        chunk(2 * j, st_a, mx_a, st_b, mx_b)
        chunk(2 * j + 1, st_b, mx_b, st_a, mx_a)
        return carry

    lax.fori_loop(0, nk // 2, pair, 0)
    _attn_finish(o_ref, l_ref, acc_ref)


def _attn_finish(o_ref, l_ref, acc_ref):
    for g in range(Q_PER_KV):
        out_t = acc_ref[g] * (1.0 / l_ref[g])
        o_ref[:, g * HEAD_DIM:(g + 1) * HEAD_DIM] = out_t.T.astype(o_ref.dtype)


def _attn_bounded_body(qt_ref, k_ref, vt_ref, o_ref, l_ref, acc_ref, *, tk, nk):
    l_ref[...] = jnp.zeros(l_ref.shape, F32)
    acc_ref[...] = jnp.zeros(acc_ref.shape, F32)

    chunks_per_step = next(n for n in (8, 4, 2) if nk % n == 0)

    def step(j, carry):
        pending = None
        for u in range(chunks_per_step * Q_PER_KV):
            c, g = chunks_per_step * j + u // Q_PER_KV, u % Q_PER_KV
            kc = k_ref[pl.ds(pl.multiple_of(c * tk, tk), tk), :]
            st = jnp.dot(kc, qt_ref[g * HEAD_DIM:(g + 1) * HEAD_DIM, :],
                         preferred_element_type=F32)
            if pending is not None:
                pc, pg, pt = pending
                acc_ref[pg] += jnp.dot(vt_ref[0, pc], pt, preferred_element_type=F32)
            pt = jnp.exp2(st)
            l_ref[g] += jnp.sum(pt, axis=0, keepdims=True)
            pending = (c, g, pt.astype(BF16))
        pc, pg, pt = pending
        acc_ref[pg] += jnp.dot(vt_ref[0, pc], pt, preferred_element_type=F32)
        return carry

    lax.fori_loop(0, nk // chunks_per_step, step, 0)
    _attn_finish(o_ref, l_ref, acc_ref)


def _attention(qt, k, vt, *, bounded, tq=512):
    s_len = k.shape[0]
    nk, tk = vt.shape[1], vt.shape[3]
    tq = min(tq, s_len)
    assert nk * tk == s_len and nk % 2 == 0 and s_len % tq == 0
    stat = pltpu.VMEM((Q_PER_KV, 1, tq), F32)
    acc = pltpu.VMEM((Q_PER_KV, HEAD_DIM, tq), F32)
    if bounded:
        body = functools.partial(_attn_bounded_body, tk=tk, nk=nk)
        scratch = [stat, acc]
    else:
        body = functools.partial(_attn_body, tk=tk, nk=nk)
        chunk = pltpu.VMEM((Q_PER_KV, tk, tq), F32)
        scratch = [chunk, chunk, stat, stat, stat, stat, acc]
    return pl.pallas_call(
        body,
        grid=(N_KV_HEADS, s_len // tq),
        in_specs=[pl.BlockSpec((GROUP_WIDTH, tq), lambda h, i: (h, i)),
                  pl.BlockSpec((s_len, HEAD_DIM), lambda h, i: (0, h)),
                  pl.BlockSpec((1, nk, HEAD_DIM, tk), lambda h, i: (h, 0, 0, 0))],
        out_specs=pl.BlockSpec((tq, GROUP_WIDTH), lambda h, i: (i, h)),
        out_shape=jax.ShapeDtypeStruct((s_len, D_MODEL), BF16),
        scratch_shapes=scratch,
        compiler_params=_params(("parallel", "arbitrary"), 48),
    )(qt, k, vt)


def _score_bound(q_gain, k_gain, q_scale):
    return 1.02 * HEAD_DIM * q_scale * jnp.max(jnp.abs(q_gain)) * jnp.max(jnp.abs(k_gain))


def _mem_body(qm_ref, mk_ref, mv_ref, y_hbm_ref, o_ref):
    del y_hbm_ref
    scale = MEM_HEAD_DIM ** -0.5
    for h in range(MEM_HEADS):
        cols = slice(h * MEM_HEAD_DIM, (h + 1) * MEM_HEAD_DIM)
        s = lax.dot_general(qm_ref[:, cols], mk_ref[:, cols], NT_DIMS,
                            preferred_element_type=F32) * scale
        p = jnp.exp(s - jnp.max(s, axis=-1, keepdims=True))
        p = p * (1.0 / jnp.sum(p, axis=-1, keepdims=True))
        o_ref[:, cols] = jnp.dot(p.astype(BF16), mv_ref[:, cols],
                                 preferred_element_type=F32).astype(o_ref.dtype)


def _mem_attention(qm_arr, qm_block, mem_kv, y, *, tm=1024):
    s_len = y.shape[0]
    tm = min(tm, s_len)
    out_block = MIXER_WIDTH // MEM_WIDTH
    return pl.pallas_call(
        _mem_body,
        grid=(s_len // tm,),
        in_specs=[pl.BlockSpec((tm, MEM_WIDTH), lambda i: (i, qm_block)),
                  pl.BlockSpec((N_MEM, MEM_WIDTH), lambda i: (0, 0)),
                  pl.BlockSpec((N_MEM, MEM_WIDTH), lambda i: (0, 1)),
                  pl.BlockSpec(memory_space=pl.ANY)],
        out_specs=pl.BlockSpec((tm, MEM_WIDTH), lambda i: (i, out_block)),
        out_shape=jax.ShapeDtypeStruct(y.shape, y.dtype),
        input_output_aliases={3: 0},
        compiler_params=_params(("parallel",), 32),
    )(qm_arr, mem_kv, mem_kv, y)


def _pool_body(up_ref, um_ref, un_ref, pw_ref, sc_ref, o_ref, *, tm, s_len):
    i = pl.program_id(0)
    g = pl.program_id(1)
    lo = jnp.left_shift(jnp.int32(1), g)
    hi = lo - 1
    u_ext = jnp.concatenate([up_ref[...], um_ref[...], un_ref[...]], axis=0)
    ext = tm + 2 * POOL_HALO
    t = lax.broadcasted_iota(jnp.int32, (tm, ext), 0)
    j = lax.broadcasted_iota(jnp.int32, (tm, ext), 1)
    d = j - POOL_HALO - t
    r = i * tm + j - POOL_HALO
    in_window = jnp.where(d >= -lo, 1.0, 0.0)
    in_window = jnp.where(d <= hi, in_window, 0.0)
    in_window = jnp.where(r >= 0, in_window, 0.0)
    in_window = jnp.where(r < s_len, in_window, 0.0)
    t_col = i * tm + lax.broadcasted_iota(jnp.int32, (tm, 1), 0)
    cnt = jnp.minimum(t_col + hi + 1, s_len) - jnp.maximum(t_col - lo, 0)
    cnt_f = cnt.astype(F32)
    band = jnp.where(d == 0, 1.0 - cnt_f, in_window).astype(BF16)
    pooled = jnp.dot(band, u_ext, preferred_element_type=F32) * (1.0 / cnt_f)
    y = jnp.dot(pooled.astype(BF16), pw_ref[0], preferred_element_type=F32) * sc_ref[...]
    o_ref[...] = y.astype(o_ref.dtype)


def _pool_mix(proj, pool_w, pool_scale, *, tm=512):
    s_len = proj.shape[0]
    tm = min(tm, s_len)
    hb = tm // POOL_HALO
    last = s_len // POOL_HALO - 1
    body = functools.partial(_pool_body, tm=tm, s_len=s_len)
    return pl.pallas_call(
        body,
        grid=(s_len // tm, 4),
        in_specs=[pl.BlockSpec((POOL_HALO, POOL_GROUP), lambda i, g: (jnp.maximum(i * hb - 1, 0), g)),
                  pl.BlockSpec((tm, POOL_GROUP), lambda i, g: (i, g)),
                  pl.BlockSpec((POOL_HALO, POOL_GROUP), lambda i, g: (jnp.minimum((i + 1) * hb, last), g)),
                  pl.BlockSpec((1, POOL_GROUP, POOL_GROUP), lambda i, g: (g, 0, 0)),
                  pl.BlockSpec((1, POOL_GROUP), lambda i, g: (0, g))],
        out_specs=pl.BlockSpec((tm, POOL_GROUP), lambda i, g: (i, g)),
        out_shape=jax.ShapeDtypeStruct((s_len, D_MODEL), BF16),
        compiler_params=_params(("parallel", "arbitrary"), 32),
    )(proj, proj, proj, pool_w, pool_scale)


def _deepnorm(h, y, gamma, beta):
    z = ALPHA * h + y
    mu = jnp.mean(z, axis=-1, keepdims=True)
    zc = z - mu
    var = jnp.mean(zc * zc, axis=-1, keepdims=True)
    return zc * lax.rsqrt(var + LN_EPS) * gamma + beta


def _deepnorm_rows(hb_ref, g_ref, b_ref, y_ref, o_ref):
    def rows_step(r, carry):
        rows = pl.ds(pl.multiple_of(r * LN_ROWS, LN_ROWS), LN_ROWS)
        out = _deepnorm(hb_ref[rows, :].astype(F32), y_ref[rows, :], g_ref[...], b_ref[...])
        o_ref[rows, :] = out.astype(o_ref.dtype)
        return carry

    lax.fori_loop(0, o_ref.shape[0] // LN_ROWS, rows_step, 0)


def _ln_body(y_ref, hb_ref, g_ref, b_ref, o_ref):
    out = _deepnorm(hb_ref[...].astype(F32), y_ref[...], g_ref[...], b_ref[...])
    o_ref[...] = out.astype(o_ref.dtype)


def _residual_ln(y, hb, gamma, beta, *, tm=256):
    m, d = hb.shape
    tm = min(tm, m)
    row = pl.BlockSpec((tm, d), lambda i: (i, 0))
    vec = pl.BlockSpec((1, d), lambda i: (0, 0))
    return pl.pallas_call(
        _ln_body,
        grid=(m // tm,),
        in_specs=[row, row, vec, vec],
        out_specs=row,
        out_shape=jax.ShapeDtypeStruct((m, d), BF16),
        compiler_params=_params(("parallel",), 32),
    )(y, hb, gamma, beta)


def _mlp_body(hb_ref, w1_ref, w2_ref, g_ref, b_ref, o_ref, *scratch, nf):
    acc_ref = scratch[0] if scratch else o_ref
    f = pl.program_id(1)

    @pl.when(f == 0)
    def _():
        acc_ref[...] = jnp.zeros_like(acc_ref)

    a = jnp.dot(hb_ref[...], w1_ref[...], preferred_element_type=F32)
    a = jnp.square(jnp.maximum(a, 0.0)).astype(BF16)
    for n in range(0, acc_ref.shape[1], MLP_OUT_CHUNK):
        cols = slice(n, n + MLP_OUT_CHUNK)
        acc_ref[:, cols] += jnp.dot(a, w2_ref[:, cols], preferred_element_type=F32)

    @pl.when(f == nf - 1)
    def _():
        _deepnorm_rows(hb_ref, g_ref, b_ref, acc_ref, o_ref)


def _mlp_ln(hb, w1, w2, gamma, beta, *, out_dtype, tm=1024, tf=512):
    m, d = hb.shape
    dff = w1.shape[1]
    tm = min(tm, m)
    body = functools.partial(_mlp_body, nf=dff // tf)
    row = lambda i, f: (i, 0)
    scratch = [] if out_dtype == F32 else [pltpu.VMEM((tm, d), F32)]
    return pl.pallas_call(
        body,
        grid=(m // tm, dff // tf),
        in_specs=[pl.BlockSpec((tm, d), row, pipeline_mode=pl.Buffered(1)),
                  pl.BlockSpec((d, tf), lambda i, f: (0, f)),
                  pl.BlockSpec((tf, d), lambda i, f: (f, 0)),
                  pl.BlockSpec((1, d), lambda i, f: (0, 0)),
                  pl.BlockSpec((1, d), lambda i, f: (0, 0))],
        out_specs=pl.BlockSpec((tm, d), row, pipeline_mode=pl.Buffered(1)),
        out_shape=jax.ShapeDtypeStruct((m, d), out_dtype),
        scratch_shapes=scratch,
        compiler_params=_params(("parallel", "arbitrary"), 60),
    )(hb, w1, w2, gamma, beta)


def _rope_tables(s_len):
    rows = s_len // GRID_W
    inv_freq = ROPE_THETA ** (-jnp.arange(N_FREQ, dtype=F32) / N_FREQ)
    row_ang = jnp.arange(rows, dtype=F32)[:, None, None] * inv_freq
    col_ang = jnp.arange(GRID_W, dtype=F32)[None, :, None] * inv_freq
    ang = jnp.stack([jnp.broadcast_to(row_ang, (rows, GRID_W, N_FREQ)),
                     jnp.broadcast_to(col_ang, (rows, GRID_W, N_FREQ))], axis=2).reshape(s_len, 2, 1, N_FREQ)
    cos_t = jnp.broadcast_to(jnp.cos(ang), (s_len, 2, 2, N_FREQ)).reshape(s_len, HEAD_DIM)
    sign = jnp.array([-1.0, 1.0], F32).reshape(1, 1, 2, 1)
    sin_t = (jnp.sin(ang) * sign).reshape(s_len, HEAD_DIM)
    return cos_t, sin_t


def _vec(p):
    return p.reshape(1, -1).astype(F32)


def kernel(x, mem, w_mem_kv, l0_w_in, l0_q_gain, l0_k_gain, l0_w_out, l0_ln1_g, l0_ln1_b, l0_w_ff1, l0_w_ff2, l0_ln2_g, l0_ln2_b, l1_w_in, l1_pool_w, l1_pool_scale, l1_w_out, l1_ln1_g, l1_ln1_b, l1_w_ff1, l1_w_ff2, l1_ln2_g, l1_ln2_b, l2_w_in, l2_q_gain, l2_k_gain, l2_w_out, l2_ln1_g, l2_ln1_b, l2_w_ff1, l2_w_ff2, l2_ln2_g, l2_ln2_b, l3_w_in, l3_pool_w, l3_pool_scale, l3_w_out, l3_ln1_g, l3_ln1_b, l3_w_ff1, l3_w_ff2, l3_ln2_g, l3_ln2_b):
    layers = [
        (l0_w_in, l0_q_gain, l0_k_gain, l0_w_out, l0_ln1_g, l0_ln1_b, l0_w_ff1, l0_w_ff2, l0_ln2_g, l0_ln2_b),
        (l1_w_in, l1_pool_w, l1_pool_scale, l1_w_out, l1_ln1_g, l1_ln1_b, l1_w_ff1, l1_w_ff2, l1_ln2_g, l1_ln2_b),
        (l2_w_in, l2_q_gain, l2_k_gain, l2_w_out, l2_ln1_g, l2_ln1_b, l2_w_ff1, l2_w_ff2, l2_ln2_g, l2_ln2_b),
        (l3_w_in, l3_pool_w, l3_pool_scale, l3_w_out, l3_ln1_g, l3_ln1_b, l3_w_ff1, l3_w_ff2, l3_ln2_g, l3_ln2_b),
    ]
    b, s_len, d = x.shape
    assert b == 1 and d == D_MODEL

    mem_kv = _matmul(mem[0].astype(BF16), w_mem_kv.astype(BF16),
                     col0=0, ncols=2 * MEM_WIDTH, tn=MEM_WIDTH, out_dtype=BF16)
    cos_t, sin_t = _rope_tables(s_len)
    q_scale = HEAD_DIM ** -0.5 * math.log2(math.e)
    cos_q, sin_q = cos_t * q_scale, sin_t * q_scale

    hb = x[0].astype(BF16)
    for i in range(DEPTH):
        w_in, pa, pb, w_out, ln1_g, ln1_b, w_ff1, w_ff2, ln2_g, ln2_b = layers[i]
        w_in = w_in.astype(BF16)
        if i % 2 == 0:
            qt = _qk_proj(hb, w_in, _vec(pa), cos_q, sin_q, col0=0, ncols=MIXER_WIDTH, tn=512,
                          transpose_out=True)
            k = _qk_proj(hb, w_in, _vec(pb), cos_t, sin_t, col0=MIXER_WIDTH, ncols=KV_WIDTH, tn=KV_WIDTH)
            vt = _v_proj(hb, w_in, col0=MIXER_WIDTH + KV_WIDTH, tk=min(ATTN_KEY_CHUNK, s_len // 2))
            qm = _matmul(hb, w_in, col0=MIXER_WIDTH + 2 * KV_WIDTH, ncols=MEM_WIDTH, tn=512, out_dtype=BF16)
            y = lax.cond(_score_bound(pa, pb, q_scale) <= SCORE_BOUND_LIMIT,
                         functools.partial(_attention, bounded=True),
                         functools.partial(_attention, bounded=False),
                         qt, k, vt)
            y = _mem_attention(qm, 0, mem_kv, y)
        else:
            proj = _matmul(hb, w_in, col0=0, ncols=D_MODEL, tn=1024, out_dtype=BF16)
            y = _pool_mix(proj, pa.astype(BF16), _vec(pb))
            y = _mem_attention(proj, MIXER_WIDTH // MEM_WIDTH, mem_kv, y)
        y = _matmul(y, w_out.astype(BF16), col0=0, ncols=D_MODEL, tn=1024, out_dtype=BF16)
        hb = _residual_ln(y, hb, _vec(ln1_g), _vec(ln1_b))
        hb = _mlp_ln(hb, w_ff1.astype(BF16), w_ff2.astype(BF16), _vec(ln2_g), _vec(ln2_b),
                     out_dtype=F32 if i == DEPTH - 1 else BF16)
    return hb[None]
```

```python
import functools
import math

import jax
import jax.numpy as jnp
from jax import lax
from jax.experimental import pallas as pl
from jax.experimental.pallas import tpu as pltpu

D_MODEL = 4096
DEPTH = 4
GRID_W = 64
N_MEM = 256
MEM_HEADS = 4
MEM_HEAD_DIM = 256
MEM_WIDTH = MEM_HEADS * MEM_HEAD_DIM
MIXER_WIDTH = D_MODEL - MEM_WIDTH
HEAD_DIM = 128
N_KV_HEADS = 6
Q_PER_KV = 4
GROUP_WIDTH = Q_PER_KV * HEAD_DIM
KV_WIDTH = N_KV_HEADS * HEAD_DIM
ATTN_KEY_CHUNK = 512
CAST_HEAD_STEPS = 4
SCORE_BOUND_LIMIT = 64.0
ROPE_THETA = 10000.0
N_FREQ = HEAD_DIM // 4
POOL_GROUP = MIXER_WIDTH // 4
POOL_HALO = 16
D_FF = 4 * D_MODEL
MLP_OUT_CHUNK = 512
LN_ROWS = 64
ALPHA = (2 * DEPTH) ** 0.25
LN_EPS = 1e-5
RMS_EPS = 1e-6

MIB = 1024 * 1024
BF16 = jnp.bfloat16
F32 = jnp.float32
NT_DIMS = (((1,), (1,)), ((), ()))


def _params(semantics, vmem_mib):
    return pltpu.CompilerParams(dimension_semantics=semantics, vmem_limit_bytes=vmem_mib * MIB)


def _mm_body(a_ref, b_ref, o_ref):
    o_ref[...] = jnp.dot(a_ref[...], b_ref[...], preferred_element_type=F32).astype(o_ref.dtype)


def _matmul(a, b, *, col0, ncols, tn, out_dtype, tm=1024):
    m, k = a.shape
    tm = min(tm, m)
    nb0 = col0 // tn
    assert col0 % tn == 0 and ncols % tn == 0 and m % tm == 0
    return pl.pallas_call(
        _mm_body,
        grid=(m // tm, ncols // tn),
        in_specs=[pl.BlockSpec((tm, k), lambda i, j: (i, 0)),
                  pl.BlockSpec((k, tn), lambda i, j: (0, j + nb0))],
        out_specs=pl.BlockSpec((tm, tn), lambda i, j: (i, j)),
        out_shape=jax.ShapeDtypeStruct((m, ncols), out_dtype),
        compiler_params=_params(("parallel", "arbitrary"), 48),
    )(a, b)


def _v_body(a_ref, b_ref, o_ref, *, tk):
    acc_t = jnp.dot(a_ref[...], b_ref[...], preferred_element_type=F32).T
    for h in range(N_KV_HEADS):
        for c in range(o_ref.shape[1]):
            o_ref[h, c] = acc_t[h * HEAD_DIM:(h + 1) * HEAD_DIM,
                                c * tk:(c + 1) * tk].astype(o_ref.dtype)


def _v_proj(a, b, *, col0, tk, tm=1024):
    m, k = a.shape
    tm = min(tm, m)
    assert col0 % KV_WIDTH == 0 and m % tm == 0 and tm % tk == 0
    nb0 = col0 // KV_WIDTH
    return pl.pallas_call(
        functools.partial(_v_body, tk=tk),
        grid=(m // tm,),
        in_specs=[pl.BlockSpec((tm, k), lambda i: (i, 0)),
                  pl.BlockSpec((k, KV_WIDTH), lambda i: (0, nb0))],
        out_specs=pl.BlockSpec((N_KV_HEADS, tm // tk, HEAD_DIM, tk), lambda i: (0, i, 0, 0)),
        out_shape=jax.ShapeDtypeStruct((N_KV_HEADS, m // tk, HEAD_DIM, tk), BF16),
        compiler_params=_params(("parallel",), 48),
    )(a, b)


def _qk_body(a_ref, b_ref, gain_ref, cos_ref, sin_ref, o_ref, raw_a, raw_b, *, transpose_out):
    s = pl.program_id(0)

    def epilogue(raw_ref):
        tm, tn = raw_ref.shape
        lane = lax.broadcasted_iota(jnp.int32, (tm, HEAD_DIM), 1)
        first_half = (lane & (N_FREQ)) == 0
        gain = gain_ref[...]
        cos = cos_ref[...]
        sin = sin_ref[...]
        for j in range(tn // HEAD_DIM):
            x = raw_ref[:, j * HEAD_DIM:(j + 1) * HEAD_DIM]
            ms = jnp.mean(x * x, axis=-1, keepdims=True)
            xn = x * lax.rsqrt(ms + RMS_EPS) * gain
            partner = jnp.where(first_half,
                                pltpu.roll(xn, HEAD_DIM - N_FREQ, 1),
                                pltpu.roll(xn, N_FREQ, 1))
            out = xn * cos + partner * sin
            if transpose_out:
                o_ref[j * HEAD_DIM:(j + 1) * HEAD_DIM, :] = out.T.astype(o_ref.dtype)
            else:
                o_ref[:, j * HEAD_DIM:(j + 1) * HEAD_DIM] = out.astype(o_ref.dtype)

    def step(raw_dst, raw_src):
        raw_dst[...] = jnp.dot(a_ref[...], b_ref[...], preferred_element_type=F32)
        epilogue(raw_src)

    @pl.when(s == 0)
    def _():
        raw_b[...] = jnp.zeros_like(raw_b)

    @pl.when(s % 2 == 0)
    def _():
        step(raw_a, raw_b)

    @pl.when(s % 2 == 1)
    def _():
        step(raw_b, raw_a)


def _qk_proj(a, b, gain, cos_t, sin_t, *, col0, ncols, tn, tm=1024, transpose_out=False):
    m, k = a.shape
    tm = min(tm, m)
    nb0 = col0 // tn
    assert col0 % tn == 0 and ncols % tn == 0 and m % tm == 0
    nj = ncols // tn
    tiles = (m // tm) * nj

    def mm_tile(s):
        return jnp.minimum(s, tiles - 1)

    def ep_tile(s):
        return jnp.maximum(s - 1, 0)

    if transpose_out:
        out_spec = pl.BlockSpec((tn, tm), lambda s: (ep_tile(s) % nj, ep_tile(s) // nj))
        out_shape = jax.ShapeDtypeStruct((ncols, m), BF16)
    else:
        out_spec = pl.BlockSpec((tm, tn), lambda s: (ep_tile(s) // nj, ep_tile(s) % nj))
        out_shape = jax.ShapeDtypeStruct((m, ncols), BF16)
    table = pl.BlockSpec((tm, HEAD_DIM), lambda s: (ep_tile(s) // nj, 0))
    return pl.pallas_call(
        functools.partial(_qk_body, transpose_out=transpose_out),
        grid=(tiles + 1,),
        in_specs=[pl.BlockSpec((tm, k), lambda s: (mm_tile(s) // nj, 0)),
                  pl.BlockSpec((k, tn), lambda s: (0, mm_tile(s) % nj + nb0)),
                  pl.BlockSpec((1, HEAD_DIM), lambda s: (0, 0)),
                  table, table],
        out_specs=out_spec,
        out_shape=out_shape,
        scratch_shapes=[pltpu.VMEM((tm, tn), F32), pltpu.VMEM((tm, tn), F32)],
        compiler_params=_params(("arbitrary",), 56),
    )(a, b, gain, cos_t, sin_t)


def _attn_body(qt_ref, k_ref, vt_ref, o_ref, st_a, st_b, mx_a, mx_b, m_ref, l_ref, acc_ref, *, tk, nk):
    def scores(kc, g, dst, mx_dst):
        st = jnp.dot(kc, qt_ref[g * HEAD_DIM:(g + 1) * HEAD_DIM, :],
                     preferred_element_type=F32)
        dst[g] = st
        mx_dst[g] = jnp.max(st, axis=0, keepdims=True)

    def key_chunk(c):
        return k_ref[pl.ds(pl.multiple_of(c * tk, tk), tk), :]

    def chunk(c, src, mx_src, dst, mx_dst):
        kc_next = key_chunk(jnp.minimum(c + 1, nk - 1))
        vtc = vt_ref[0, c]
        for g in range(Q_PER_KV):
            scores(kc_next, g, dst, mx_dst)
            m = m_ref[g]
            m_new = jnp.maximum(m, mx_src[g])
            alpha = jnp.exp2(m - m_new)
            pt = jnp.exp2(src[g] - m_new)
            m_ref[g] = m_new
            l_ref[g] = alpha * l_ref[g] + jnp.sum(pt, axis=0, keepdims=True)
            acc_ref[g] = alpha * acc_ref[g] + jnp.dot(vtc, pt.astype(BF16),
                                                      preferred_element_type=F32)

    m_ref[...] = jnp.full(m_ref.shape, -jnp.inf, F32)
    l_ref[...] = jnp.zeros(l_ref.shape, F32)
    acc_ref[...] = jnp.zeros(acc_ref.shape, F32)
    for g in range(Q_PER_KV):
        scores(key_chunk(0), g, st_a, mx_a)

    def pair(j, carry):
        chunk(2 * j, st_a, mx_a, st_b, mx_b)
        chunk(2 * j + 1, st_b, mx_b, st_a, mx_a)
        return carry

    lax.fori_loop(0, nk // 2, pair, 0)
    _attn_finish(o_ref, l_ref, acc_ref)


def _attn_finish(o_ref, l_ref, acc_ref):
    for g in range(Q_PER_KV):
        out_t = acc_ref[g] * (1.0 / l_ref[g])
        o_ref[:, g * HEAD_DIM:(g + 1) * HEAD_DIM] = out_t.T.astype(o_ref.dtype)


def _attn_bounded_body(qt_ref, k_ref, vt_ref, o_ref, l_ref, acc_ref, *, tk, nk):
    l_ref[...] = jnp.zeros(l_ref.shape, F32)
    acc_ref[...] = jnp.zeros(acc_ref.shape, F32)

    chunks_per_step = next(n for n in (8, 4, 2) if nk % n == 0)

    def step(j, carry):
        pending = None
        for u in range(chunks_per_step * Q_PER_KV):
            c, g = chunks_per_step * j + u // Q_PER_KV, u % Q_PER_KV
            kc = k_ref[pl.ds(pl.multiple_of(c * tk, tk), tk), :]
            st = jnp.dot(kc, qt_ref[g * HEAD_DIM:(g + 1) * HEAD_DIM, :],
                         preferred_element_type=F32)
            if pending is not None:
                pc, pg, pt = pending
                acc_ref[pg] += jnp.dot(vt_ref[0, pc], pt, preferred_element_type=F32)
            pt = jnp.exp2(st)
            l_ref[g] += jnp.sum(pt, axis=0, keepdims=True)
            pending = (c, g, pt.astype(BF16))
        pc, pg, pt = pending
        acc_ref[pg] += jnp.dot(vt_ref[0, pc], pt, preferred_element_type=F32)
        return carry

    lax.fori_loop(0, nk // chunks_per_step, step, 0)
    _attn_finish(o_ref, l_ref, acc_ref)


def _attn_with_casts(attn_body, n_cast, *refs):
    qt_ref, k_ref, vt_ref = refs[:3]
    w_refs = refs[3:3 + n_cast]
    o_ref = refs[3 + n_cast]
    wb_refs = refs[4 + n_cast:4 + 2 * n_cast]
    attn_body(qt_ref, k_ref, vt_ref, o_ref, *refs[4 + 2 * n_cast:])
    for w_ref, wb_ref in zip(w_refs, wb_refs):
        wb_ref[...] = w_ref[...].astype(wb_ref.dtype)


def _attention(qt, k, vt, *weights, bounded, tq=512):
    s_len = k.shape[0]
    nk, tk = vt.shape[1], vt.shape[3]
    tq = min(tq, s_len)
    assert nk * tk == s_len and nk % 2 == 0 and s_len % tq == 0
    stat = pltpu.VMEM((Q_PER_KV, 1, tq), F32)
    acc = pltpu.VMEM((Q_PER_KV, HEAD_DIM, tq), F32)
    if bounded:
        body = functools.partial(_attn_bounded_body, tk=tk, nk=nk)
        scratch = [stat, acc]
    else:
        body = functools.partial(_attn_body, tk=tk, nk=nk)
        chunk = pltpu.VMEM((Q_PER_KV, tk, tq), F32)
        scratch = [chunk, chunk, stat, stat, stat, stat, acc]
    n_q = s_len // tq
    slabs = CAST_HEAD_STEPS * n_q

    def slab_index(h, i):
        return (jnp.where(h < CAST_HEAD_STEPS, h * n_q + i, slabs - 1), 0, 0)

    views = [w.reshape(slabs, w.shape[0] // slabs, w.shape[1]) for w in weights]
    slab_specs = [pl.BlockSpec((1,) + v.shape[1:], slab_index) for v in views]
    outs = pl.pallas_call(
        functools.partial(_attn_with_casts, body, len(weights)),
        grid=(N_KV_HEADS, n_q),
        in_specs=[pl.BlockSpec((GROUP_WIDTH, tq), lambda h, i: (h, i)),
                  pl.BlockSpec((s_len, HEAD_DIM), lambda h, i: (0, h)),
                  pl.BlockSpec((1, nk, HEAD_DIM, tk), lambda h, i: (h, 0, 0, 0))] + slab_specs,
        out_specs=[pl.BlockSpec((tq, GROUP_WIDTH), lambda h, i: (i, h))] + slab_specs,
        out_shape=[jax.ShapeDtypeStruct((s_len, D_MODEL), BF16)]
        + [jax.ShapeDtypeStruct(v.shape, BF16) for v in views],
        scratch_shapes=scratch,
        compiler_params=_params(("arbitrary", "arbitrary"), 56),
    )(qt, k, vt, *views)
    return (outs[0],) + tuple(o.reshape(w.shape) for o, w in zip(outs[1:], weights))


def _score_bound(q_gain, k_gain, q_scale):
    return 1.02 * HEAD_DIM * q_scale * jnp.max(jnp.abs(q_gain)) * jnp.max(jnp.abs(k_gain))


def _mem_body(qm_ref, mk_ref, mv_ref, y_hbm_ref, o_ref):
    del y_hbm_ref
    scale = MEM_HEAD_DIM ** -0.5
    for h in range(MEM_HEADS):
        cols = slice(h * MEM_HEAD_DIM, (h + 1) * MEM_HEAD_DIM)
        s = lax.dot_general(qm_ref[:, cols], mk_ref[:, cols], NT_DIMS,
                            preferred_element_type=F32) * scale
        p = jnp.exp(s - jnp.max(s, axis=-1, keepdims=True))
        p = p * (1.0 / jnp.sum(p, axis=-1, keepdims=True))
        o_ref[:, cols] = jnp.dot(p.astype(BF16), mv_ref[:, cols],
                                 preferred_element_type=F32).astype(o_ref.dtype)


def _mem_attention(qm_arr, qm_block, mem_kv, y, *, tm=1024):
    s_len = y.shape[0]
    tm = min(tm, s_len)
    out_block = MIXER_WIDTH // MEM_WIDTH
    return pl.pallas_call(
        _mem_body,
        grid=(s_len // tm,),
        in_specs=[pl.BlockSpec((tm, MEM_WIDTH), lambda i: (i, qm_block)),
                  pl.BlockSpec((N_MEM, MEM_WIDTH), lambda i: (0, 0)),
                  pl.BlockSpec((N_MEM, MEM_WIDTH), lambda i: (0, 1)),
                  pl.BlockSpec(memory_space=pl.ANY)],
        out_specs=pl.BlockSpec((tm, MEM_WIDTH), lambda i: (i, out_block)),
        out_shape=jax.ShapeDtypeStruct(y.shape, y.dtype),
        input_output_aliases={3: 0},
        compiler_params=_params(("parallel",), 32),
    )(qm_arr, mem_kv, mem_kv, y)


def _pool_body(up_ref, um_ref, un_ref, pw_ref, sc_ref, o_ref, *, tm, s_len):
    i = pl.program_id(0)
    g = pl.program_id(1)
    lo = jnp.left_shift(jnp.int32(1), g)
    hi = lo - 1
    u_ext = jnp.concatenate([up_ref[...], um_ref[...], un_ref[...]], axis=0)
    ext = tm + 2 * POOL_HALO
    t = lax.broadcasted_iota(jnp.int32, (tm, ext), 0)
    j = lax.broadcasted_iota(jnp.int32, (tm, ext), 1)
    d = j - POOL_HALO - t
    r = i * tm + j - POOL_HALO
    in_window = jnp.where(d >= -lo, 1.0, 0.0)
    in_window = jnp.where(d <= hi, in_window, 0.0)
    in_window = jnp.where(r >= 0, in_window, 0.0)
    in_window = jnp.where(r < s_len, in_window, 0.0)
    t_col = i * tm + lax.broadcasted_iota(jnp.int32, (tm, 1), 0)
    cnt = jnp.minimum(t_col + hi + 1, s_len) - jnp.maximum(t_col - lo, 0)
    cnt_f = cnt.astype(F32)
    band = jnp.where(d == 0, 1.0 - cnt_f, in_window).astype(BF16)
    pooled = jnp.dot(band, u_ext, preferred_element_type=F32) * (1.0 / cnt_f)
    y = jnp.dot(pooled.astype(BF16), pw_ref[0], preferred_element_type=F32) * sc_ref[...]
    o_ref[...] = y.astype(o_ref.dtype)


def _pool_mix(proj, pool_w, pool_scale, *, tm=512):
    s_len = proj.shape[0]
    tm = min(tm, s_len)
    hb = tm // POOL_HALO
    last = s_len // POOL_HALO - 1
    body = functools.partial(_pool_body, tm=tm, s_len=s_len)
    return pl.pallas_call(
        body,
        grid=(s_len // tm, 4),
        in_specs=[pl.BlockSpec((POOL_HALO, POOL_GROUP), lambda i, g: (jnp.maximum(i * hb - 1, 0), g)),
                  pl.BlockSpec((tm, POOL_GROUP), lambda i, g: (i, g)),
                  pl.BlockSpec((POOL_HALO, POOL_GROUP), lambda i, g: (jnp.minimum((i + 1) * hb, last), g)),
                  pl.BlockSpec((1, POOL_GROUP, POOL_GROUP), lambda i, g: (g, 0, 0)),
                  pl.BlockSpec((1, POOL_GROUP), lambda i, g: (0, g))],
        out_specs=pl.BlockSpec((tm, POOL_GROUP), lambda i, g: (i, g)),
        out_shape=jax.ShapeDtypeStruct((s_len, D_MODEL), BF16),
        compiler_params=_params(("parallel", "arbitrary"), 32),
    )(proj, proj, proj, pool_w, pool_scale)


def _deepnorm(h, y, gamma, beta):
    z = ALPHA * h + y
    mu = jnp.mean(z, axis=-1, keepdims=True)
    zc = z - mu
    var = jnp.mean(zc * zc, axis=-1, keepdims=True)
    return zc * lax.rsqrt(var + LN_EPS) * gamma + beta


def _deepnorm_rows(hb_ref, g_ref, b_ref, y_ref, o_ref):
    def rows_step(r, carry):
        rows = pl.ds(pl.multiple_of(r * LN_ROWS, LN_ROWS), LN_ROWS)
        out = _deepnorm(hb_ref[rows, :].astype(F32), y_ref[rows, :], g_ref[...], b_ref[...])
        o_ref[rows, :] = out.astype(o_ref.dtype)
        return carry

    lax.fori_loop(0, o_ref.shape[0] // LN_ROWS, rows_step, 0)


def _ln_body(y_ref, hb_ref, g_ref, b_ref, o_ref):
    out = _deepnorm(hb_ref[...].astype(F32), y_ref[...], g_ref[...], b_ref[...])
    o_ref[...] = out.astype(o_ref.dtype)


def _residual_ln(y, hb, gamma, beta, *, tm=256):
    m, d = hb.shape
    tm = min(tm, m)
    row = pl.BlockSpec((tm, d), lambda i: (i, 0))
    vec = pl.BlockSpec((1, d), lambda i: (0, 0))
    return pl.pallas_call(
        _ln_body,
        grid=(m // tm,),
        in_specs=[row, row, vec, vec],
        out_specs=row,
        out_shape=jax.ShapeDtypeStruct((m, d), BF16),
        compiler_params=_params(("parallel",), 32),
    )(y, hb, gamma, beta)


def _mlp_body(hb_ref, w1_ref, w2_ref, g_ref, b_ref, o_ref, *scratch, nf):
    acc_ref = scratch[0] if scratch else o_ref
    f = pl.program_id(1)

    @pl.when(f == 0)
    def _():
        acc_ref[...] = jnp.zeros_like(acc_ref)

    a = jnp.dot(hb_ref[...], w1_ref[...], preferred_element_type=F32)
    a = jnp.square(jnp.maximum(a, 0.0)).astype(BF16)
    for n in range(0, acc_ref.shape[1], MLP_OUT_CHUNK):
        cols = slice(n, n + MLP_OUT_CHUNK)
        acc_ref[:, cols] += jnp.dot(a, w2_ref[:, cols], preferred_element_type=F32)

    @pl.when(f == nf - 1)
    def _():
        _deepnorm_rows(hb_ref, g_ref, b_ref, acc_ref, o_ref)


def _mlp_ln(hb, w1, w2, gamma, beta, *, out_dtype, tm=1024, tf=512):
    m, d = hb.shape
    dff = w1.shape[1]
    tm = min(tm, m)
    body = functools.partial(_mlp_body, nf=dff // tf)
    row = lambda i, f: (i, 0)
    scratch = [] if out_dtype == F32 else [pltpu.VMEM((tm, d), F32)]
    return pl.pallas_call(
        body,
        grid=(m // tm, dff // tf),
        in_specs=[pl.BlockSpec((tm, d), row, pipeline_mode=pl.Buffered(1)),
                  pl.BlockSpec((d, tf), lambda i, f: (0, f)),
                  pl.BlockSpec((tf, d), lambda i, f: (f, 0)),
                  pl.BlockSpec((1, d), lambda i, f: (0, 0)),
                  pl.BlockSpec((1, d), lambda i, f: (0, 0))],
        out_specs=pl.BlockSpec((tm, d), row, pipeline_mode=pl.Buffered(1)),
        out_shape=jax.ShapeDtypeStruct((m, d), out_dtype),
        scratch_shapes=scratch,
        compiler_params=_params(("parallel", "arbitrary"), 60),
    )(hb, w1, w2, gamma, beta)


def _rope_tables(s_len):
    rows = s_len // GRID_W
    inv_freq = ROPE_THETA ** (-jnp.arange(N_FREQ, dtype=F32) / N_FREQ)
    row_ang = jnp.arange(rows, dtype=F32)[:, None, None] * inv_freq
    col_ang = jnp.arange(GRID_W, dtype=F32)[None, :, None] * inv_freq
    ang = jnp.stack([jnp.broadcast_to(row_ang, (rows, GRID_W, N_FREQ)),
                     jnp.broadcast_to(col_ang, (rows, GRID_W, N_FREQ))], axis=2).reshape(s_len, 2, 1, N_FREQ)
    cos_t = jnp.broadcast_to(jnp.cos(ang), (s_len, 2, 2, N_FREQ)).reshape(s_len, HEAD_DIM)
    sign = jnp.array([-1.0, 1.0], F32).reshape(1, 1, 2, 1)
    sin_t = (jnp.sin(ang) * sign).reshape(s_len, HEAD_DIM)
    return cos_t, sin_t


def _vec(p):
    return p.reshape(1, -1).astype(F32)


def kernel(x, mem, w_mem_kv, l0_w_in, l0_q_gain, l0_k_gain, l0_w_out, l0_ln1_g, l0_ln1_b, l0_w_ff1, l0_w_ff2, l0_ln2_g, l0_ln2_b, l1_w_in, l1_pool_w, l1_pool_scale, l1_w_out, l1_ln1_g, l1_ln1_b, l1_w_ff1, l1_w_ff2, l1_ln2_g, l1_ln2_b, l2_w_in, l2_q_gain, l2_k_gain, l2_w_out, l2_ln1_g, l2_ln1_b, l2_w_ff1, l2_w_ff2, l2_ln2_g, l2_ln2_b, l3_w_in, l3_pool_w, l3_pool_scale, l3_w_out, l3_ln1_g, l3_ln1_b, l3_w_ff1, l3_w_ff2, l3_ln2_g, l3_ln2_b):
    layers = [
        (l0_w_in, l0_q_gain, l0_k_gain, l0_w_out, l0_ln1_g, l0_ln1_b, l0_w_ff1, l0_w_ff2, l0_ln2_g, l0_ln2_b),
        (l1_w_in, l1_pool_w, l1_pool_scale, l1_w_out, l1_ln1_g, l1_ln1_b, l1_w_ff1, l1_w_ff2, l1_ln2_g, l1_ln2_b),
        (l2_w_in, l2_q_gain, l2_k_gain, l2_w_out, l2_ln1_g, l2_ln1_b, l2_w_ff1, l2_w_ff2, l2_ln2_g, l2_ln2_b),
        (l3_w_in, l3_pool_w, l3_pool_scale, l3_w_out, l3_ln1_g, l3_ln1_b, l3_w_ff1, l3_w_ff2, l3_ln2_g, l3_ln2_b),
    ]
    b, s_len, d = x.shape
    assert b == 1 and d == D_MODEL

    mem_kv = _matmul(mem[0].astype(BF16), w_mem_kv.astype(BF16),
                     col0=0, ncols=2 * MEM_WIDTH, tn=MEM_WIDTH, out_dtype=BF16)
    cos_t, sin_t = _rope_tables(s_len)
    q_scale = HEAD_DIM ** -0.5 * math.log2(math.e)
    cos_q, sin_q = cos_t * q_scale, sin_t * q_scale

    hb = x[0].astype(BF16)
    for i in range(DEPTH):
        w_in, pa, pb, w_out, ln1_g, ln1_b, w_ff1, w_ff2, ln2_g, ln2_b = layers[i]
        w_in = w_in.astype(BF16)
        if i % 2 == 0:
            qt = _qk_proj(hb, w_in, _vec(pa), cos_q, sin_q, col0=0, ncols=MIXER_WIDTH, tn=512,
                          transpose_out=True)
            k = _qk_proj(hb, w_in, _vec(pb), cos_t, sin_t, col0=MIXER_WIDTH, ncols=KV_WIDTH, tn=KV_WIDTH)
            vt = _v_proj(hb, w_in, col0=MIXER_WIDTH + KV_WIDTH, tk=min(ATTN_KEY_CHUNK, s_len // 2))
            qm = _matmul(hb, w_in, col0=MIXER_WIDTH + 2 * KV_WIDTH, ncols=MEM_WIDTH, tn=512, out_dtype=BF16)
            mlp_f32 = [w for layer in layers[i:i + 2] for w in layer[6:8]]
            y, *mlp_bf16 = lax.cond(_score_bound(pa, pb, q_scale) <= SCORE_BOUND_LIMIT,
                                    functools.partial(_attention, bounded=True),
                                    functools.partial(_attention, bounded=False),
                                    qt, k, vt, *mlp_f32)
            y = _mem_attention(qm, 0, mem_kv, y)
        else:
            proj = _matmul(hb, w_in, col0=0, ncols=D_MODEL, tn=1024, out_dtype=BF16)
            y = _pool_mix(proj, pa.astype(BF16), _vec(pb))
            y = _mem_attention(proj, MIXER_WIDTH // MEM_WIDTH, mem_kv, y)
        y = _matmul(y, w_out.astype(BF16), col0=0, ncols=D_MODEL, tn=1024, out_dtype=BF16)
        hb = _residual_ln(y, hb, _vec(ln1_g), _vec(ln1_b))
        w1b, w2b = mlp_bf16[2 * (i % 2):2 * (i % 2) + 2]
        hb = _mlp_ln(hb, w1b, w2b, _vec(ln2_g), _vec(ln2_b),
                     out_dtype=F32 if i == DEPTH - 1 else BF16)
    return hb[None]
```

```python
import functools
import math

import jax
import jax.numpy as jnp
from jax import lax
from jax.experimental import pallas as pl
from jax.experimental.pallas import tpu as pltpu

D_MODEL = 4096
DEPTH = 4
GRID_W = 64
N_MEM = 256
MEM_HEADS = 4
MEM_HEAD_DIM = 256
MEM_WIDTH = MEM_HEADS * MEM_HEAD_DIM
MIXER_WIDTH = D_MODEL - MEM_WIDTH
HEAD_DIM = 128
N_KV_HEADS = 6
Q_PER_KV = 4
GROUP_WIDTH = Q_PER_KV * HEAD_DIM
KV_WIDTH = N_KV_HEADS * HEAD_DIM
ATTN_KEY_CHUNK = 512
CAST_HEAD_STEPS = 4
SCORE_BOUND_LIMIT = 64.0
ROPE_THETA = 10000.0
N_FREQ = HEAD_DIM // 4
POOL_GROUP = MIXER_WIDTH // 4
POOL_HALO = 16
D_FF = 4 * D_MODEL
MLP_OUT_CHUNK = 512
LN_ROWS = 64
ALPHA = (2 * DEPTH) ** 0.25
LN_EPS = 1e-5
RMS_EPS = 1e-6

MIB = 1024 * 1024
BF16 = jnp.bfloat16
F32 = jnp.float32
NT_DIMS = (((1,), (1,)), ((), ()))


def _params(semantics, vmem_mib):
    return pltpu.CompilerParams(dimension_semantics=semantics, vmem_limit_bytes=vmem_mib * MIB)


def _mm_body(a_ref, b_ref, o_ref):
    o_ref[...] = jnp.dot(a_ref[...], b_ref[...], preferred_element_type=F32).astype(o_ref.dtype)


def _matmul(a, b, *, col0, ncols, tn, out_dtype, tm=1024):
    m, k = a.shape
    tm = min(tm, m)
    nb0 = col0 // tn
    assert col0 % tn == 0 and ncols % tn == 0 and m % tm == 0
    return pl.pallas_call(
        _mm_body,
        grid=(m // tm, ncols // tn),
        in_specs=[pl.BlockSpec((tm, k), lambda i, j: (i, 0)),
                  pl.BlockSpec((k, tn), lambda i, j: (0, j + nb0))],
        out_specs=pl.BlockSpec((tm, tn), lambda i, j: (i, j)),
        out_shape=jax.ShapeDtypeStruct((m, ncols), out_dtype),
        compiler_params=_params(("parallel", "arbitrary"), 48),
    )(a, b)


def _v_body(a_ref, b_ref, o_ref, *, tk):
    acc_t = jnp.dot(a_ref[...], b_ref[...], preferred_element_type=F32).T
    for h in range(N_KV_HEADS):
        for c in range(o_ref.shape[1]):
            o_ref[h, c] = acc_t[h * HEAD_DIM:(h + 1) * HEAD_DIM,
                                c * tk:(c + 1) * tk].astype(o_ref.dtype)


def _v_proj(a, b, *, col0, tk, tm=1024):
    m, k = a.shape
    tm = min(tm, m)
    assert col0 % KV_WIDTH == 0 and m % tm == 0 and tm % tk == 0
    nb0 = col0 // KV_WIDTH
    return pl.pallas_call(
        functools.partial(_v_body, tk=tk),
        grid=(m // tm,),
        in_specs=[pl.BlockSpec((tm, k), lambda i: (i, 0)),
                  pl.BlockSpec((k, KV_WIDTH), lambda i: (0, nb0))],
        out_specs=pl.BlockSpec((N_KV_HEADS, tm // tk, HEAD_DIM, tk), lambda i: (0, i, 0, 0)),
        out_shape=jax.ShapeDtypeStruct((N_KV_HEADS, m // tk, HEAD_DIM, tk), BF16),
        compiler_params=_params(("parallel",), 48),
    )(a, b)


def _qk_body(a_ref, b_ref, gain_ref, cos_ref, sin_ref, o_ref, raw_a, raw_b, *, transpose_out):
    s = pl.program_id(0)

    def epilogue(raw_ref):
        tm, tn = raw_ref.shape
        lane = lax.broadcasted_iota(jnp.int32, (tm, HEAD_DIM), 1)
        first_half = (lane & (N_FREQ)) == 0
        gain = gain_ref[...]
        cos = cos_ref[...]
        sin = sin_ref[...]
        for j in range(tn // HEAD_DIM):
            x = raw_ref[:, j * HEAD_DIM:(j + 1) * HEAD_DIM]
            ms = jnp.mean(x * x, axis=-1, keepdims=True)
            xn = x * lax.rsqrt(ms + RMS_EPS) * gain
            partner = jnp.where(first_half,
                                pltpu.roll(xn, HEAD_DIM - N_FREQ, 1),
                                pltpu.roll(xn, N_FREQ, 1))
            out = xn * cos + partner * sin
            if transpose_out:
                o_ref[j * HEAD_DIM:(j + 1) * HEAD_DIM, :] = out.T.astype(o_ref.dtype)
            else:
                o_ref[:, j * HEAD_DIM:(j + 1) * HEAD_DIM] = out.astype(o_ref.dtype)

    def step(raw_dst, raw_src):
        raw_dst[...] = jnp.dot(a_ref[...], b_ref[...], preferred_element_type=F32)
        epilogue(raw_src)

    @pl.when(s == 0)
    def _():
        raw_b[...] = jnp.zeros_like(raw_b)

    @pl.when(s % 2 == 0)
    def _():
        step(raw_a, raw_b)

    @pl.when(s % 2 == 1)
    def _():
        step(raw_b, raw_a)


def _qk_proj(a, b, gain, cos_t, sin_t, *, col0, ncols, tn, tm=1024, transpose_out=False):
    m, k = a.shape
    tm = min(tm, m)
    nb0 = col0 // tn
    assert col0 % tn == 0 and ncols % tn == 0 and m % tm == 0
    nj = ncols // tn
    tiles = (m // tm) * nj

    def mm_tile(s):
        return jnp.minimum(s, tiles - 1)

    def ep_tile(s):
        return jnp.maximum(s - 1, 0)

    if transpose_out:
        out_spec = pl.BlockSpec((tn, tm), lambda s: (ep_tile(s) % nj, ep_tile(s) // nj))
        out_shape = jax.ShapeDtypeStruct((ncols, m), BF16)
    else:
        out_spec = pl.BlockSpec((tm, tn), lambda s: (ep_tile(s) // nj, ep_tile(s) % nj))
        out_shape = jax.ShapeDtypeStruct((m, ncols), BF16)
    table = pl.BlockSpec((tm, HEAD_DIM), lambda s: (ep_tile(s) // nj, 0))
    return pl.pallas_call(
        functools.partial(_qk_body, transpose_out=transpose_out),
        grid=(tiles + 1,),
        in_specs=[pl.BlockSpec((tm, k), lambda s: (mm_tile(s) // nj, 0)),
                  pl.BlockSpec((k, tn), lambda s: (0, mm_tile(s) % nj + nb0)),
                  pl.BlockSpec((1, HEAD_DIM), lambda s: (0, 0)),
                  table, table],
        out_specs=out_spec,
        out_shape=out_shape,
        scratch_shapes=[pltpu.VMEM((tm, tn), F32), pltpu.VMEM((tm, tn), F32)],
        compiler_params=_params(("arbitrary",), 56),
    )(a, b, gain, cos_t, sin_t)


def _attn_body(qt_ref, k_ref, vt_ref, o_ref, st_a, st_b, mx_a, mx_b, m_ref, l_ref, acc_ref, *, tk, nk):
    def scores(kc, g, dst, mx_dst):
        st = jnp.dot(kc, qt_ref[g * HEAD_DIM:(g + 1) * HEAD_DIM, :],
                     preferred_element_type=F32)
        dst[g] = st
        mx_dst[g] = jnp.max(st, axis=0, keepdims=True)

    def key_chunk(c):
        return k_ref[pl.ds(pl.multiple_of(c * tk, tk), tk), :]

    def chunk(c, src, mx_src, dst, mx_dst):
        kc_next = key_chunk(jnp.minimum(c + 1, nk - 1))
        vtc = vt_ref[0, c]
        for g in range(Q_PER_KV):
            scores(kc_next, g, dst, mx_dst)
            m = m_ref[g]
            m_new = jnp.maximum(m, mx_src[g])
            alpha = jnp.exp2(m - m_new)
            pt = jnp.exp2(src[g] - m_new)
            m_ref[g] = m_new
            l_ref[g] = alpha * l_ref[g] + jnp.sum(pt, axis=0, keepdims=True)
            acc_ref[g] = alpha * acc_ref[g] + jnp.dot(vtc, pt.astype(BF16),
                                                      preferred_element_type=F32)

    m_ref[...] = jnp.full(m_ref.shape, -jnp.inf, F32)
    l_ref[...] = jnp.zeros(l_ref.shape, F32)
    acc_ref[...] = jnp.zeros(acc_ref.shape, F32)
    for g in range(Q_PER_KV):
        scores(key_chunk(0), g, st_a, mx_a)

    def pair(j, carry):
        chunk(2 * j, st_a, mx_a, st_b, mx_b)
        chunk(2 * j + 1, st_b, mx_b, st_a, mx_a)
        return carry

    lax.fori_loop(0, nk // 2, pair, 0)
    _attn_finish(o_ref, l_ref, acc_ref)


def _attn_finish(o_ref, l_ref, acc_ref):
    for g in range(Q_PER_KV):
        out_t = acc_ref[g] * (1.0 / l_ref[g])
        o_ref[:, g * HEAD_DIM:(g + 1) * HEAD_DIM] = out_t.T.astype(o_ref.dtype)


def _attn_bounded_body(qt_ref, k_ref, vt_ref, o_ref, l_ref, acc_ref, *, tk, nk):
    l_ref[...] = jnp.zeros(l_ref.shape, F32)
    acc_ref[...] = jnp.zeros(acc_ref.shape, F32)

    chunks_per_step = next(n for n in (8, 4, 2) if nk % n == 0)

    def step(j, carry):
        pending = None
        for u in range(chunks_per_step * Q_PER_KV):
            c, g = chunks_per_step * j + u // Q_PER_KV, u % Q_PER_KV
            kc = k_ref[pl.ds(pl.multiple_of(c * tk, tk), tk), :]
            st = jnp.dot(kc, qt_ref[g * HEAD_DIM:(g + 1) * HEAD_DIM, :],
                         preferred_element_type=F32)
            if pending is not None:
                pc, pg, pt = pending
                acc_ref[pg] += jnp.dot(vt_ref[0, pc], pt, preferred_element_type=F32)
            pt = jnp.exp2(st)
            l_ref[g] += jnp.sum(pt, axis=0, keepdims=True)
            pending = (c, g, pt.astype(BF16))
        pc, pg, pt = pending
        acc_ref[pg] += jnp.dot(vt_ref[0, pc], pt, preferred_element_type=F32)
        return carry

    lax.fori_loop(0, nk // chunks_per_step, step, 0)
    _attn_finish(o_ref, l_ref, acc_ref)


def _attn_with_casts(attn_body, n_cast, *refs):
    qt_ref, k_ref, vt_ref = refs[:3]
    w_refs = refs[3:3 + n_cast]
    o_ref = refs[3 + n_cast]
    wb_refs = refs[4 + n_cast:4 + 2 * n_cast]
    attn_body(qt_ref, k_ref, vt_ref, o_ref, *refs[4 + 2 * n_cast:])
    for w_ref, wb_ref in zip(w_refs, wb_refs):
        wb_ref[...] = w_ref[...].astype(wb_ref.dtype)


def _attention(qt, k, vt, *weights, bounded, tq=512):
    s_len = k.shape[0]
    nk, tk = vt.shape[1], vt.shape[3]
    tq = min(tq, s_len)
    assert nk * tk == s_len and nk % 2 == 0 and s_len % tq == 0
    stat = pltpu.VMEM((Q_PER_KV, 1, tq), F32)
    acc = pltpu.VMEM((Q_PER_KV, HEAD_DIM, tq), F32)
    if bounded:
        body = functools.partial(_attn_bounded_body, tk=tk, nk=nk)
        scratch = [stat, acc]
    else:
        body = functools.partial(_attn_body, tk=tk, nk=nk)
        chunk = pltpu.VMEM((Q_PER_KV, tk, tq), F32)
        scratch = [chunk, chunk, stat, stat, stat, stat, acc]
    n_q = s_len // tq
    slabs = CAST_HEAD_STEPS * n_q

    def slab_index(h, i):
        return (jnp.where(h < CAST_HEAD_STEPS, h * n_q + i, slabs - 1), 0, 0)

    views = [w.reshape(slabs, w.shape[0] // slabs, w.shape[1]) for w in weights]
    slab_specs = [pl.BlockSpec((1,) + v.shape[1:], slab_index) for v in views]
    outs = pl.pallas_call(
        functools.partial(_attn_with_casts, body, len(weights)),
        grid=(N_KV_HEADS, n_q),
        in_specs=[pl.BlockSpec((GROUP_WIDTH, tq), lambda h, i: (h, i)),
                  pl.BlockSpec((s_len, HEAD_DIM), lambda h, i: (0, h)),
                  pl.BlockSpec((1, nk, HEAD_DIM, tk), lambda h, i: (h, 0, 0, 0))] + slab_specs,
        out_specs=[pl.BlockSpec((tq, GROUP_WIDTH), lambda h, i: (i, h))] + slab_specs,
        out_shape=[jax.ShapeDtypeStruct((s_len, D_MODEL), BF16)]
        + [jax.ShapeDtypeStruct(v.shape, BF16) for v in views],
        scratch_shapes=scratch,
        compiler_params=_params(("arbitrary", "arbitrary"), 62),
    )(qt, k, vt, *views)
    return (outs[0],) + tuple(o.reshape(w.shape) for o, w in zip(outs[1:], weights))


def _score_bound(q_gain, k_gain, q_scale):
    return 1.02 * HEAD_DIM * q_scale * jnp.max(jnp.abs(q_gain)) * jnp.max(jnp.abs(k_gain))


def _mem_body(qm_ref, mk_ref, mv_ref, y_hbm_ref, o_ref):
    del y_hbm_ref
    scale = MEM_HEAD_DIM ** -0.5
    for h in range(MEM_HEADS):
        cols = slice(h * MEM_HEAD_DIM, (h + 1) * MEM_HEAD_DIM)
        s = lax.dot_general(qm_ref[:, cols], mk_ref[:, cols], NT_DIMS,
                            preferred_element_type=F32) * scale
        p = jnp.exp(s - jnp.max(s, axis=-1, keepdims=True))
        p = p * (1.0 / jnp.sum(p, axis=-1, keepdims=True))
        o_ref[:, cols] = jnp.dot(p.astype(BF16), mv_ref[:, cols],
                                 preferred_element_type=F32).astype(o_ref.dtype)


def _mem_attention(qm_arr, qm_block, mem_kv, y, *, tm=1024):
    s_len = y.shape[0]
    tm = min(tm, s_len)
    out_block = MIXER_WIDTH // MEM_WIDTH
    return pl.pallas_call(
        _mem_body,
        grid=(s_len // tm,),
        in_specs=[pl.BlockSpec((tm, MEM_WIDTH), lambda i: (i, qm_block)),
                  pl.BlockSpec((N_MEM, MEM_WIDTH), lambda i: (0, 0)),
                  pl.BlockSpec((N_MEM, MEM_WIDTH), lambda i: (0, 1)),
                  pl.BlockSpec(memory_space=pl.ANY)],
        out_specs=pl.BlockSpec((tm, MEM_WIDTH), lambda i: (i, out_block)),
        out_shape=jax.ShapeDtypeStruct(y.shape, y.dtype),
        input_output_aliases={3: 0},
        compiler_params=_params(("parallel",), 32),
    )(qm_arr, mem_kv, mem_kv, y)


def _pool_body(up_ref, um_ref, un_ref, pw_ref, sc_ref, o_ref, *, tm, s_len):
    i = pl.program_id(0)
    g = pl.program_id(1)
    lo = jnp.left_shift(jnp.int32(1), g)
    hi = lo - 1
    u_ext = jnp.concatenate([up_ref[...], um_ref[...], un_ref[...]], axis=0)
    ext = tm + 2 * POOL_HALO
    t = lax.broadcasted_iota(jnp.int32, (tm, ext), 0)
    j = lax.broadcasted_iota(jnp.int32, (tm, ext), 1)
    d = j - POOL_HALO - t
    r = i * tm + j - POOL_HALO
    in_window = jnp.where(d >= -lo, 1.0, 0.0)
    in_window = jnp.where(d <= hi, in_window, 0.0)
    in_window = jnp.where(r >= 0, in_window, 0.0)
    in_window = jnp.where(r < s_len, in_window, 0.0)
    t_col = i * tm + lax.broadcasted_iota(jnp.int32, (tm, 1), 0)
    cnt = jnp.minimum(t_col + hi + 1, s_len) - jnp.maximum(t_col - lo, 0)
    cnt_f = cnt.astype(F32)
    band = jnp.where(d == 0, 1.0 - cnt_f, in_window).astype(BF16)
    pooled = jnp.dot(band, u_ext, preferred_element_type=F32) * (1.0 / cnt_f)
    y = jnp.dot(pooled.astype(BF16), pw_ref[0], preferred_element_type=F32) * sc_ref[...]
    o_ref[...] = y.astype(o_ref.dtype)


def _pool_mix(proj, pool_w, pool_scale, *, tm=512):
    s_len = proj.shape[0]
    tm = min(tm, s_len)
    hb = tm // POOL_HALO
    last = s_len // POOL_HALO - 1
    body = functools.partial(_pool_body, tm=tm, s_len=s_len)
    return pl.pallas_call(
        body,
        grid=(s_len // tm, 4),
        in_specs=[pl.BlockSpec((POOL_HALO, POOL_GROUP), lambda i, g: (jnp.maximum(i * hb - 1, 0), g)),
                  pl.BlockSpec((tm, POOL_GROUP), lambda i, g: (i, g)),
                  pl.BlockSpec((POOL_HALO, POOL_GROUP), lambda i, g: (jnp.minimum((i + 1) * hb, last), g)),
                  pl.BlockSpec((1, POOL_GROUP, POOL_GROUP), lambda i, g: (g, 0, 0)),
                  pl.BlockSpec((1, POOL_GROUP), lambda i, g: (0, g))],
        out_specs=pl.BlockSpec((tm, POOL_GROUP), lambda i, g: (i, g)),
        out_shape=jax.ShapeDtypeStruct((s_len, D_MODEL), BF16),
        compiler_params=_params(("parallel", "arbitrary"), 32),
    )(proj, proj, proj, pool_w, pool_scale)


def _deepnorm(h, y, gamma, beta):
    z = ALPHA * h + y
    mu = jnp.mean(z, axis=-1, keepdims=True)
    zc = z - mu
    var = jnp.mean(zc * zc, axis=-1, keepdims=True)
    return zc * lax.rsqrt(var + LN_EPS) * gamma + beta


def _deepnorm_rows(hb_ref, g_ref, b_ref, y_ref, o_ref):
    def rows_step(r, carry):
        rows = pl.ds(pl.multiple_of(r * LN_ROWS, LN_ROWS), LN_ROWS)
        out = _deepnorm(hb_ref[rows, :].astype(F32), y_ref[rows, :], g_ref[...], b_ref[...])
        o_ref[rows, :] = out.astype(o_ref.dtype)
        return carry

    lax.fori_loop(0, o_ref.shape[0] // LN_ROWS, rows_step, 0)


def _ln_body(y_ref, hb_ref, g_ref, b_ref, o_ref):
    out = _deepnorm(hb_ref[...].astype(F32), y_ref[...], g_ref[...], b_ref[...])
    o_ref[...] = out.astype(o_ref.dtype)


def _residual_ln(y, hb, gamma, beta, *, tm=256):
    m, d = hb.shape
    tm = min(tm, m)
    row = pl.BlockSpec((tm, d), lambda i: (i, 0))
    vec = pl.BlockSpec((1, d), lambda i: (0, 0))
    return pl.pallas_call(
        _ln_body,
        grid=(m // tm,),
        in_specs=[row, row, vec, vec],
        out_specs=row,
        out_shape=jax.ShapeDtypeStruct((m, d), BF16),
        compiler_params=_params(("parallel",), 32),
    )(y, hb, gamma, beta)


def _mlp_body(hb_ref, w1_ref, w2_ref, g_ref, b_ref, o_ref, *scratch, nf):
    acc_ref = scratch[0] if scratch else o_ref
    f = pl.program_id(1)

    @pl.when(f == 0)
    def _():
        acc_ref[...] = jnp.zeros_like(acc_ref)

    a = jnp.dot(hb_ref[...], w1_ref[...], preferred_element_type=F32)
    a = jnp.square(jnp.maximum(a, 0.0)).astype(BF16)
    for n in range(0, acc_ref.shape[1], MLP_OUT_CHUNK):
        cols = slice(n, n + MLP_OUT_CHUNK)
        acc_ref[:, cols] += jnp.dot(a, w2_ref[:, cols], preferred_element_type=F32)

    @pl.when(f == nf - 1)
    def _():
        _deepnorm_rows(hb_ref, g_ref, b_ref, acc_ref, o_ref)


def _mlp_ln(hb, w1, w2, gamma, beta, *, out_dtype, tm=1024, tf=512):
    m, d = hb.shape
    dff = w1.shape[1]
    tm = min(tm, m)
    body = functools.partial(_mlp_body, nf=dff // tf)
    row = lambda i, f: (i, 0)
    scratch = [] if out_dtype == F32 else [pltpu.VMEM((tm, d), F32)]
    return pl.pallas_call(
        body,
        grid=(m // tm, dff // tf),
        in_specs=[pl.BlockSpec((tm, d), row, pipeline_mode=pl.Buffered(1)),
                  pl.BlockSpec((d, tf), lambda i, f: (0, f)),
                  pl.BlockSpec((tf, d), lambda i, f: (f, 0)),
                  pl.BlockSpec((1, d), lambda i, f: (0, 0)),
                  pl.BlockSpec((1, d), lambda i, f: (0, 0))],
        out_specs=pl.BlockSpec((tm, d), row, pipeline_mode=pl.Buffered(1)),
        out_shape=jax.ShapeDtypeStruct((m, d), out_dtype),
        scratch_shapes=scratch,
        compiler_params=_params(("parallel", "arbitrary"), 60),
    )(hb, w1, w2, gamma, beta)


def _rope_tables(s_len):
    rows = s_len // GRID_W
    inv_freq = ROPE_THETA ** (-jnp.arange(N_FREQ, dtype=F32) / N_FREQ)
    row_ang = jnp.arange(rows, dtype=F32)[:, None, None] * inv_freq
    col_ang = jnp.arange(GRID_W, dtype=F32)[None, :, None] * inv_freq
    ang = jnp.stack([jnp.broadcast_to(row_ang, (rows, GRID_W, N_FREQ)),
                     jnp.broadcast_to(col_ang, (rows, GRID_W, N_FREQ))], axis=2).reshape(s_len, 2, 1, N_FREQ)
    cos_t = jnp.broadcast_to(jnp.cos(ang), (s_len, 2, 2, N_FREQ)).reshape(s_len, HEAD_DIM)
    sign = jnp.array([-1.0, 1.0], F32).reshape(1, 1, 2, 1)
    sin_t = (jnp.sin(ang) * sign).reshape(s_len, HEAD_DIM)
    return cos_t, sin_t


def _vec(p):
    return p.reshape(1, -1).astype(F32)


def kernel(x, mem, w_mem_kv, l0_w_in, l0_q_gain, l0_k_gain, l0_w_out, l0_ln1_g, l0_ln1_b, l0_w_ff1, l0_w_ff2, l0_ln2_g, l0_ln2_b, l1_w_in, l1_pool_w, l1_pool_scale, l1_w_out, l1_ln1_g, l1_ln1_b, l1_w_ff1, l1_w_ff2, l1_ln2_g, l1_ln2_b, l2_w_in, l2_q_gain, l2_k_gain, l2_w_out, l2_ln1_g, l2_ln1_b, l2_w_ff1, l2_w_ff2, l2_ln2_g, l2_ln2_b, l3_w_in, l3_pool_w, l3_pool_scale, l3_w_out, l3_ln1_g, l3_ln1_b, l3_w_ff1, l3_w_ff2, l3_ln2_g, l3_ln2_b):
    layers = [
        (l0_w_in, l0_q_gain, l0_k_gain, l0_w_out, l0_ln1_g, l0_ln1_b, l0_w_ff1, l0_w_ff2, l0_ln2_g, l0_ln2_b),
        (l1_w_in, l1_pool_w, l1_pool_scale, l1_w_out, l1_ln1_g, l1_ln1_b, l1_w_ff1, l1_w_ff2, l1_ln2_g, l1_ln2_b),
        (l2_w_in, l2_q_gain, l2_k_gain, l2_w_out, l2_ln1_g, l2_ln1_b, l2_w_ff1, l2_w_ff2, l2_ln2_g, l2_ln2_b),
        (l3_w_in, l3_pool_w, l3_pool_scale, l3_w_out, l3_ln1_g, l3_ln1_b, l3_w_ff1, l3_w_ff2, l3_ln2_g, l3_ln2_b),
    ]
    b, s_len, d = x.shape
    assert b == 1 and d == D_MODEL

    mem_kv = _matmul(mem[0].astype(BF16), w_mem_kv.astype(BF16),
                     col0=0, ncols=2 * MEM_WIDTH, tn=MEM_WIDTH, out_dtype=BF16)
    cos_t, sin_t = _rope_tables(s_len)
    q_scale = HEAD_DIM ** -0.5 * math.log2(math.e)
    cos_q, sin_q = cos_t * q_scale, sin_t * q_scale

    hb = x[0].astype(BF16)
    w_in_slot, w_out_slot, w_ff1_slot, w_ff2_slot = 0, 3, 6, 7
    cast = {}

    def weight(layer, slot):
        if (layer, slot) in cast:
            return cast[(layer, slot)]
        return layers[layer][slot].astype(BF16)

    for i in range(DEPTH):
        _, pa, pb, _, ln1_g, ln1_b, _, _, ln2_g, ln2_b = layers[i]
        w_in = weight(i, w_in_slot)
        if i % 2 == 0:
            qt = _qk_proj(hb, w_in, _vec(pa), cos_q, sin_q, col0=0, ncols=MIXER_WIDTH, tn=512,
                          transpose_out=True)
            k = _qk_proj(hb, w_in, _vec(pb), cos_t, sin_t, col0=MIXER_WIDTH, ncols=KV_WIDTH, tn=KV_WIDTH)
            vt = _v_proj(hb, w_in, col0=MIXER_WIDTH + KV_WIDTH, tk=min(ATTN_KEY_CHUNK, s_len // 2))
            qm = _matmul(hb, w_in, col0=MIXER_WIDTH + 2 * KV_WIDTH, ncols=MEM_WIDTH, tn=512, out_dtype=BF16)
            wanted = [(i, w_out_slot), (i, w_ff1_slot), (i, w_ff2_slot)]
            wanted += [(i + 1, s) for s in (w_in_slot, w_out_slot, w_ff1_slot, w_ff2_slot)]
            if i + 2 < DEPTH:
                wanted.append((i + 2, w_in_slot))
            y, *done = lax.cond(_score_bound(pa, pb, q_scale) <= SCORE_BOUND_LIMIT,
                                functools.partial(_attention, bounded=True),
                                functools.partial(_attention, bounded=False),
                                qt, k, vt, *[layers[l][s] for l, s in wanted])
            cast.update(zip(wanted, done))
            y = _mem_attention(qm, 0, mem_kv, y)
        else:
            proj = _matmul(hb, w_in, col0=0, ncols=D_MODEL, tn=1024, out_dtype=BF16)
            y = _pool_mix(proj, pa.astype(BF16), _vec(pb))
            y = _mem_attention(proj, MIXER_WIDTH // MEM_WIDTH, mem_kv, y)
        y = _matmul(y, weight(i, w_out_slot), col0=0, ncols=D_MODEL, tn=1024, out_dtype=BF16)
        hb = _residual_ln(y, hb, _vec(ln1_g), _vec(ln1_b))
        hb = _mlp_ln(hb, weight(i, w_ff1_slot), weight(i, w_ff2_slot), _vec(ln2_g), _vec(ln2_b),
                     out_dtype=F32 if i == DEPTH - 1 else BF16)
    return hb[None]
```

```python
import functools
import math

import jax
import jax.numpy as jnp
from jax import lax
from jax.experimental import pallas as pl
from jax.experimental.pallas import tpu as pltpu

D_MODEL = 4096
DEPTH = 4
GRID_W = 64
N_MEM = 256
MEM_HEADS = 4
MEM_HEAD_DIM = 256
MEM_WIDTH = MEM_HEADS * MEM_HEAD_DIM
MIXER_WIDTH = D_MODEL - MEM_WIDTH
HEAD_DIM = 128
N_KV_HEADS = 6
Q_PER_KV = 4
GROUP_WIDTH = Q_PER_KV * HEAD_DIM
KV_WIDTH = N_KV_HEADS * HEAD_DIM
ATTN_KEY_CHUNK = 512
CAST_HEAD_STEPS = 4
SCORE_BOUND_LIMIT = 64.0
ROPE_THETA = 10000.0
N_FREQ = HEAD_DIM // 4
POOL_GROUP = MIXER_WIDTH // 4
POOL_HALO = 16
D_FF = 4 * D_MODEL
MLP_OUT_CHUNK = 512
LN_ROWS = 64
ALPHA = (2 * DEPTH) ** 0.25
LN_EPS = 1e-5
RMS_EPS = 1e-6

MIB = 1024 * 1024
BF16 = jnp.bfloat16
F32 = jnp.float32
NT_DIMS = (((1,), (1,)), ((), ()))


def _params(semantics, vmem_mib):
    return pltpu.CompilerParams(dimension_semantics=semantics, vmem_limit_bytes=vmem_mib * MIB)


def _mm_body(a_ref, b_ref, o_ref):
    o_ref[...] = jnp.dot(a_ref[...], b_ref[...], preferred_element_type=F32).astype(o_ref.dtype)


def _matmul(a, b, *, col0, ncols, tn, out_dtype, tm=1024):
    m, k = a.shape
    tm = min(tm, m)
    nb0 = col0 // tn
    assert col0 % tn == 0 and ncols % tn == 0 and m % tm == 0
    return pl.pallas_call(
        _mm_body,
        grid=(m // tm, ncols // tn),
        in_specs=[pl.BlockSpec((tm, k), lambda i, j: (i, 0)),
                  pl.BlockSpec((k, tn), lambda i, j: (0, j + nb0))],
        out_specs=pl.BlockSpec((tm, tn), lambda i, j: (i, j)),
        out_shape=jax.ShapeDtypeStruct((m, ncols), out_dtype),
        compiler_params=_params(("parallel", "arbitrary"), 48),
    )(a, b)


def _v_body(a_ref, b_ref, o_ref, *, tk):
    acc_t = jnp.dot(a_ref[...], b_ref[...], preferred_element_type=F32).T
    for h in range(N_KV_HEADS):
        for c in range(o_ref.shape[1]):
            o_ref[h, c] = acc_t[h * HEAD_DIM:(h + 1) * HEAD_DIM,
                                c * tk:(c + 1) * tk].astype(o_ref.dtype)


def _v_proj(a, b, *, col0, tk, tm=1024):
    m, k = a.shape
    tm = min(tm, m)
    assert col0 % KV_WIDTH == 0 and m % tm == 0 and tm % tk == 0
    nb0 = col0 // KV_WIDTH
    return pl.pallas_call(
        functools.partial(_v_body, tk=tk),
        grid=(m // tm,),
        in_specs=[pl.BlockSpec((tm, k), lambda i: (i, 0)),
                  pl.BlockSpec((k, KV_WIDTH), lambda i: (0, nb0))],
        out_specs=pl.BlockSpec((N_KV_HEADS, tm // tk, HEAD_DIM, tk), lambda i: (0, i, 0, 0)),
        out_shape=jax.ShapeDtypeStruct((N_KV_HEADS, m // tk, HEAD_DIM, tk), BF16),
        compiler_params=_params(("parallel",), 48),
    )(a, b)


def _qk_body(a_ref, b_ref, gain_ref, cos_ref, sin_ref, o_ref, raw_a, raw_b, *, transpose_out):
    s = pl.program_id(0)

    def epilogue(raw_ref):
        tm, tn = raw_ref.shape
        lane = lax.broadcasted_iota(jnp.int32, (tm, HEAD_DIM), 1)
        first_half = (lane & (N_FREQ)) == 0
        gain = gain_ref[...]
        cos = cos_ref[...]
        sin = sin_ref[...]
        for j in range(tn // HEAD_DIM):
            x = raw_ref[:, j * HEAD_DIM:(j + 1) * HEAD_DIM]
            ms = jnp.mean(x * x, axis=-1, keepdims=True)
            xn = x * lax.rsqrt(ms + RMS_EPS) * gain
            partner = jnp.where(first_half,
                                pltpu.roll(xn, HEAD_DIM - N_FREQ, 1),
                                pltpu.roll(xn, N_FREQ, 1))
            out = xn * cos + partner * sin
            if transpose_out:
                o_ref[j * HEAD_DIM:(j + 1) * HEAD_DIM, :] = out.T.astype(o_ref.dtype)
            else:
                o_ref[:, j * HEAD_DIM:(j + 1) * HEAD_DIM] = out.astype(o_ref.dtype)

    def step(raw_dst, raw_src):
        raw_dst[...] = jnp.dot(a_ref[...], b_ref[...], preferred_element_type=F32)
        epilogue(raw_src)

    @pl.when(s == 0)
    def _():
        raw_b[...] = jnp.zeros_like(raw_b)

    @pl.when(s % 2 == 0)
    def _():
        step(raw_a, raw_b)

    @pl.when(s % 2 == 1)
    def _():
        step(raw_b, raw_a)


def _qk_proj(a, b, gain, cos_t, sin_t, *, col0, ncols, tn, tm=1024, transpose_out=False):
    m, k = a.shape
    tm = min(tm, m)
    nb0 = col0 // tn
    assert col0 % tn == 0 and ncols % tn == 0 and m % tm == 0
    nj = ncols // tn
    tiles = (m // tm) * nj

    def mm_tile(s):
        return jnp.minimum(s, tiles - 1)

    def ep_tile(s):
        return jnp.maximum(s - 1, 0)

    if transpose_out:
        out_spec = pl.BlockSpec((tn, tm), lambda s: (ep_tile(s) % nj, ep_tile(s) // nj))
        out_shape = jax.ShapeDtypeStruct((ncols, m), BF16)
    else:
        out_spec = pl.BlockSpec((tm, tn), lambda s: (ep_tile(s) // nj, ep_tile(s) % nj))
        out_shape = jax.ShapeDtypeStruct((m, ncols), BF16)
    table = pl.BlockSpec((tm, HEAD_DIM), lambda s: (ep_tile(s) // nj, 0))
    return pl.pallas_call(
        functools.partial(_qk_body, transpose_out=transpose_out),
        grid=(tiles + 1,),
        in_specs=[pl.BlockSpec((tm, k), lambda s: (mm_tile(s) // nj, 0)),
                  pl.BlockSpec((k, tn), lambda s: (0, mm_tile(s) % nj + nb0)),
                  pl.BlockSpec((1, HEAD_DIM), lambda s: (0, 0)),
                  table, table],
        out_specs=out_spec,
        out_shape=out_shape,
        scratch_shapes=[pltpu.VMEM((tm, tn), F32), pltpu.VMEM((tm, tn), F32)],
        compiler_params=_params(("arbitrary",), 56),
    )(a, b, gain, cos_t, sin_t)


def _attn_body(qt_ref, k_ref, vt_ref, o_ref, st_a, st_b, mx_a, mx_b, m_ref, l_ref, acc_ref, *, tk, nk):
    def scores(kc, g, dst, mx_dst):
        st = jnp.dot(kc, qt_ref[g * HEAD_DIM:(g + 1) * HEAD_DIM, :],
                     preferred_element_type=F32)
        dst[g] = st
        mx_dst[g] = jnp.max(st, axis=0, keepdims=True)

    def key_chunk(c):
        return k_ref[pl.ds(pl.multiple_of(c * tk, tk), tk), :]

    def chunk(c, src, mx_src, dst, mx_dst):
        kc_next = key_chunk(jnp.minimum(c + 1, nk - 1))
        vtc = vt_ref[0, c]
        for g in range(Q_PER_KV):
            scores(kc_next, g, dst, mx_dst)
            m = m_ref[g]
            m_new = jnp.maximum(m, mx_src[g])
            alpha = jnp.exp2(m - m_new)
            pt = jnp.exp2(src[g] - m_new)
            m_ref[g] = m_new
            l_ref[g] = alpha * l_ref[g] + jnp.sum(pt, axis=0, keepdims=True)
            acc_ref[g] = alpha * acc_ref[g] + jnp.dot(vtc, pt.astype(BF16),
                                                      preferred_element_type=F32)

    m_ref[...] = jnp.full(m_ref.shape, -jnp.inf, F32)
    l_ref[...] = jnp.zeros(l_ref.shape, F32)
    acc_ref[...] = jnp.zeros(acc_ref.shape, F32)
    for g in range(Q_PER_KV):
        scores(key_chunk(0), g, st_a, mx_a)

    def pair(j, carry):
        chunk(2 * j, st_a, mx_a, st_b, mx_b)
        chunk(2 * j + 1, st_b, mx_b, st_a, mx_a)
        return carry

    lax.fori_loop(0, nk // 2, pair, 0)
    _attn_finish(o_ref, l_ref, acc_ref)


def _attn_finish(o_ref, l_ref, acc_ref):
    for g in range(Q_PER_KV):
        out_t = acc_ref[g] * (1.0 / l_ref[g])
        o_ref[:, g * HEAD_DIM:(g + 1) * HEAD_DIM] = out_t.T.astype(o_ref.dtype)


def _attn_bounded_body(qt_ref, k_ref, vt_ref, o_ref, l_ref, acc_ref, *, tk, nk):
    l_ref[...] = jnp.zeros(l_ref.shape, F32)
    acc_ref[...] = jnp.zeros(acc_ref.shape, F32)

    chunks_per_step = next(n for n in (8, 4, 2) if nk % n == 0)

    def step(j, carry):
        pending = None
        for u in range(chunks_per_step * Q_PER_KV):
            c, g = chunks_per_step * j + u // Q_PER_KV, u % Q_PER_KV
            kc = k_ref[pl.ds(pl.multiple_of(c * tk, tk), tk), :]
            st = jnp.dot(kc, qt_ref[g * HEAD_DIM:(g + 1) * HEAD_DIM, :],
                         preferred_element_type=F32)
            if pending is not None:
                pc, pg, pt = pending
                acc_ref[pg] += jnp.dot(vt_ref[0, pc], pt, preferred_element_type=F32)
            pt = jnp.exp2(st)
            l_ref[g] += jnp.sum(pt, axis=0, keepdims=True)
            pending = (c, g, pt.astype(BF16))
        pc, pg, pt = pending
        acc_ref[pg] += jnp.dot(vt_ref[0, pc], pt, preferred_element_type=F32)
        return carry

    lax.fori_loop(0, nk // chunks_per_step, step, 0)
    _attn_finish(o_ref, l_ref, acc_ref)


def _attn_with_casts(attn_body, n_cast, *refs):
    qt_ref, k_ref, vt_ref = refs[:3]
    w_refs = refs[3:3 + n_cast]
    o_ref = refs[3 + n_cast]
    wb_refs = refs[4 + n_cast:4 + 2 * n_cast]
    attn_body(qt_ref, k_ref, vt_ref, o_ref, *refs[4 + 2 * n_cast:])
    for w_ref, wb_ref in zip(w_refs, wb_refs):
        wb_ref[...] = w_ref[...].astype(wb_ref.dtype)


def _attention(qt, k, vt, *weights, bounded, tq=512):
    s_len = k.shape[0]
    nk, tk = vt.shape[1], vt.shape[3]
    tq = min(tq, s_len)
    assert nk * tk == s_len and nk % 2 == 0 and s_len % tq == 0
    stat = pltpu.VMEM((Q_PER_KV, 1, tq), F32)
    acc = pltpu.VMEM((Q_PER_KV, HEAD_DIM, tq), F32)
    if bounded:
        body = functools.partial(_attn_bounded_body, tk=tk, nk=nk)
        scratch = [stat, acc]
    else:
        body = functools.partial(_attn_body, tk=tk, nk=nk)
        chunk = pltpu.VMEM((Q_PER_KV, tk, tq), F32)
        scratch = [chunk, chunk, stat, stat, stat, stat, acc]
    n_q = s_len // tq
    slabs = CAST_HEAD_STEPS * n_q

    def slab_index(h, i):
        return (jnp.where(h < CAST_HEAD_STEPS, h * n_q + i, slabs - 1), 0, 0)

    views = [w.reshape(slabs, w.shape[0] // slabs, w.shape[1]) for w in weights]
    slab_specs = [pl.BlockSpec((1,) + v.shape[1:], slab_index) for v in views]
    outs = pl.pallas_call(
        functools.partial(_attn_with_casts, body, len(weights)),
        grid=(N_KV_HEADS, n_q),
        in_specs=[pl.BlockSpec((GROUP_WIDTH, tq), lambda h, i: (h, i)),
                  pl.BlockSpec((s_len, HEAD_DIM), lambda h, i: (0, h)),
                  pl.BlockSpec((1, nk, HEAD_DIM, tk), lambda h, i: (h, 0, 0, 0))] + slab_specs,
        out_specs=[pl.BlockSpec((tq, GROUP_WIDTH), lambda h, i: (i, h))] + slab_specs,
        out_shape=[jax.ShapeDtypeStruct((s_len, D_MODEL), BF16)]
        + [jax.ShapeDtypeStruct(v.shape, BF16) for v in views],
        scratch_shapes=scratch,
        compiler_params=_params(("arbitrary", "arbitrary"), 62),
    )(qt, k, vt, *views)
    return (outs[0],) + tuple(o.reshape(w.shape) for o, w in zip(outs[1:], weights))


def _score_bound(q_gain, k_gain, q_scale):
    return 1.02 * HEAD_DIM * q_scale * jnp.max(jnp.abs(q_gain)) * jnp.max(jnp.abs(k_gain))


def _mem_body(qm_ref, mk_ref, mv_ref, y_hbm_ref, o_ref):
    del y_hbm_ref
    scale = MEM_HEAD_DIM ** -0.5
    for h in range(MEM_HEADS):
        cols = slice(h * MEM_HEAD_DIM, (h + 1) * MEM_HEAD_DIM)
        s = lax.dot_general(qm_ref[:, cols], mk_ref[:, cols], NT_DIMS,
                            preferred_element_type=F32) * scale
        p = jnp.exp(s - jnp.max(s, axis=-1, keepdims=True))
        p = p * (1.0 / jnp.sum(p, axis=-1, keepdims=True))
        o_ref[:, cols] = jnp.dot(p.astype(BF16), mv_ref[:, cols],
                                 preferred_element_type=F32).astype(o_ref.dtype)


def _mem_attention(qm_arr, qm_block, mem_kv, y, *, tm=1024):
    s_len = y.shape[0]
    tm = min(tm, s_len)
    out_block = MIXER_WIDTH // MEM_WIDTH
    return pl.pallas_call(
        _mem_body,
        grid=(s_len // tm,),
        in_specs=[pl.BlockSpec((tm, MEM_WIDTH), lambda i: (i, qm_block)),
                  pl.BlockSpec((N_MEM, MEM_WIDTH), lambda i: (0, 0)),
                  pl.BlockSpec((N_MEM, MEM_WIDTH), lambda i: (0, 1)),
                  pl.BlockSpec(memory_space=pl.ANY)],
        out_specs=pl.BlockSpec((tm, MEM_WIDTH), lambda i: (i, out_block)),
        out_shape=jax.ShapeDtypeStruct(y.shape, y.dtype),
        input_output_aliases={3: 0},
        compiler_params=_params(("parallel",), 32),
    )(qm_arr, mem_kv, mem_kv, y)


def _pool_body(up_ref, um_ref, un_ref, pw_ref, sc_ref, o_ref, *, tm, s_len):
    i = pl.program_id(0)
    g = pl.program_id(1)
    lo = jnp.left_shift(jnp.int32(1), g)
    hi = lo - 1
    u_ext = jnp.concatenate([up_ref[...], um_ref[...], un_ref[...]], axis=0)
    ext = tm + 2 * POOL_HALO
    t = lax.broadcasted_iota(jnp.int32, (tm, ext), 0)
    j = lax.broadcasted_iota(jnp.int32, (tm, ext), 1)
    d = j - POOL_HALO - t
    r = i * tm + j - POOL_HALO
    in_window = jnp.where(d >= -lo, 1.0, 0.0)
    in_window = jnp.where(d <= hi, in_window, 0.0)
    in_window = jnp.where(r >= 0, in_window, 0.0)
    in_window = jnp.where(r < s_len, in_window, 0.0)
    t_col = i * tm + lax.broadcasted_iota(jnp.int32, (tm, 1), 0)
    cnt = jnp.minimum(t_col + hi + 1, s_len) - jnp.maximum(t_col - lo, 0)
    cnt_f = cnt.astype(F32)
    band = jnp.where(d == 0, 1.0 - cnt_f, in_window).astype(BF16)
    pooled = jnp.dot(band, u_ext, preferred_element_type=F32) * (1.0 / cnt_f)
    y = jnp.dot(pooled.astype(BF16), pw_ref[0], preferred_element_type=F32) * sc_ref[...]
    o_ref[...] = y.astype(o_ref.dtype)


def _pool_mix(proj, pool_w, pool_scale, *, tm=512):
    s_len = proj.shape[0]
    tm = min(tm, s_len)
    hb = tm // POOL_HALO
    last = s_len // POOL_HALO - 1
    body = functools.partial(_pool_body, tm=tm, s_len=s_len)
    return pl.pallas_call(
        body,
        grid=(s_len // tm, 4),
        in_specs=[pl.BlockSpec((POOL_HALO, POOL_GROUP), lambda i, g: (jnp.maximum(i * hb - 1, 0), g)),
                  pl.BlockSpec((tm, POOL_GROUP), lambda i, g: (i, g)),
                  pl.BlockSpec((POOL_HALO, POOL_GROUP), lambda i, g: (jnp.minimum((i + 1) * hb, last), g)),
                  pl.BlockSpec((1, POOL_GROUP, POOL_GROUP), lambda i, g: (g, 0, 0)),
                  pl.BlockSpec((1, POOL_GROUP), lambda i, g: (0, g))],
        out_specs=pl.BlockSpec((tm, POOL_GROUP), lambda i, g: (i, g)),
        out_shape=jax.ShapeDtypeStruct((s_len, D_MODEL), BF16),
        compiler_params=_params(("parallel", "arbitrary"), 32),
    )(proj, proj, proj, pool_w, pool_scale)


def _deepnorm(h, y, gamma, beta):
    z = ALPHA * h + y
    mu = jnp.mean(z, axis=-1, keepdims=True)
    zc = z - mu
    var = jnp.mean(zc * zc, axis=-1, keepdims=True)
    return zc * lax.rsqrt(var + LN_EPS) * gamma + beta


def _deepnorm_rows(hb_ref, g_ref, b_ref, y_ref, o_ref):
    def rows_step(r, carry):
        rows = pl.ds(pl.multiple_of(r * LN_ROWS, LN_ROWS), LN_ROWS)
        out = _deepnorm(hb_ref[rows, :].astype(F32), y_ref[rows, :], g_ref[...], b_ref[...])
        o_ref[rows, :] = out.astype(o_ref.dtype)
        return carry

    lax.fori_loop(0, o_ref.shape[0] // LN_ROWS, rows_step, 0)


def _proj_ln_body(a_ref, w_ref, hb_ref, g_ref, b_ref, o_ref, raw_a, raw_b):
    s = pl.program_id(0)

    def step(raw_dst, raw_src):
        raw_dst[...] = jnp.dot(a_ref[...], w_ref[...], preferred_element_type=F32)
        for r in range(0, o_ref.shape[0], LN_ROWS):
            rows = slice(r, r + LN_ROWS)
            out = _deepnorm(hb_ref[rows, :].astype(F32), raw_src[rows, :], g_ref[...], b_ref[...])
            o_ref[rows, :] = out.astype(o_ref.dtype)

    @pl.when(s == 0)
    def _():
        raw_b[...] = jnp.zeros_like(raw_b)

    @pl.when(s % 2 == 0)
    def _():
        step(raw_a, raw_b)

    @pl.when(s % 2 == 1)
    def _():
        step(raw_b, raw_a)


def _proj_ln(a, w, hb, gamma, beta, *, tm=256):
    m, k = a.shape
    d = w.shape[1]
    tm = min(tm, m)
    tiles = m // tm
    prev = lambda s: (jnp.maximum(s - 1, 0), 0)
    vec = pl.BlockSpec((1, d), lambda s: (0, 0))
    return pl.pallas_call(
        _proj_ln_body,
        grid=(tiles + 1,),
        in_specs=[pl.BlockSpec((tm, k), lambda s: (jnp.minimum(s, tiles - 1), 0)),
                  pl.BlockSpec((k, d), lambda s: (0, 0), pipeline_mode=pl.Buffered(1)),
                  pl.BlockSpec((tm, d), prev),
                  vec, vec],
        out_specs=pl.BlockSpec((tm, d), prev),
        out_shape=jax.ShapeDtypeStruct((m, d), BF16),
        scratch_shapes=[pltpu.VMEM((tm, d), F32), pltpu.VMEM((tm, d), F32)],
        compiler_params=_params(("arbitrary",), 62),
    )(a, w, hb, gamma, beta)


def _mlp_body(hb_ref, w1_ref, w2_ref, g_ref, b_ref, o_ref, *scratch, nf):
    acc_ref = scratch[0] if scratch else o_ref
    f = pl.program_id(1)

    @pl.when(f == 0)
    def _():
        acc_ref[...] = jnp.zeros_like(acc_ref)

    a = jnp.dot(hb_ref[...], w1_ref[...], preferred_element_type=F32)
    a = jnp.square(jnp.maximum(a, 0.0)).astype(BF16)
    for n in range(0, acc_ref.shape[1], MLP_OUT_CHUNK):
        cols = slice(n, n + MLP_OUT_CHUNK)
        acc_ref[:, cols] += jnp.dot(a, w2_ref[:, cols], preferred_element_type=F32)

    @pl.when(f == nf - 1)
    def _():
        _deepnorm_rows(hb_ref, g_ref, b_ref, acc_ref, o_ref)


def _mlp_ln(hb, w1, w2, gamma, beta, *, out_dtype, tm=1024, tf=512):
    m, d = hb.shape
    dff = w1.shape[1]
    tm = min(tm, m)
    body = functools.partial(_mlp_body, nf=dff // tf)
    row = lambda i, f: (i, 0)
    scratch = [] if out_dtype == F32 else [pltpu.VMEM((tm, d), F32)]
    return pl.pallas_call(
        body,
        grid=(m // tm, dff // tf),
        in_specs=[pl.BlockSpec((tm, d), row, pipeline_mode=pl.Buffered(1)),
                  pl.BlockSpec((d, tf), lambda i, f: (0, f)),
                  pl.BlockSpec((tf, d), lambda i, f: (f, 0)),
                  pl.BlockSpec((1, d), lambda i, f: (0, 0)),
                  pl.BlockSpec((1, d), lambda i, f: (0, 0))],
        out_specs=pl.BlockSpec((tm, d), row, pipeline_mode=pl.Buffered(1)),
        out_shape=jax.ShapeDtypeStruct((m, d), out_dtype),
        scratch_shapes=scratch,
        compiler_params=_params(("parallel", "arbitrary"), 60),
    )(hb, w1, w2, gamma, beta)


def _rope_tables(s_len):
    rows = s_len // GRID_W
    inv_freq = ROPE_THETA ** (-jnp.arange(N_FREQ, dtype=F32) / N_FREQ)
    row_ang = jnp.arange(rows, dtype=F32)[:, None, None] * inv_freq
    col_ang = jnp.arange(GRID_W, dtype=F32)[None, :, None] * inv_freq
    ang = jnp.stack([jnp.broadcast_to(row_ang, (rows, GRID_W, N_FREQ)),
                     jnp.broadcast_to(col_ang, (rows, GRID_W, N_FREQ))], axis=2).reshape(s_len, 2, 1, N_FREQ)
    cos_t = jnp.broadcast_to(jnp.cos(ang), (s_len, 2, 2, N_FREQ)).reshape(s_len, HEAD_DIM)
    sign = jnp.array([-1.0, 1.0], F32).reshape(1, 1, 2, 1)
    sin_t = (jnp.sin(ang) * sign).reshape(s_len, HEAD_DIM)
    return cos_t, sin_t


def _vec(p):
    return p.reshape(1, -1).astype(F32)


def kernel(x, mem, w_mem_kv, l0_w_in, l0_q_gain, l0_k_gain, l0_w_out, l0_ln1_g, l0_ln1_b, l0_w_ff1, l0_w_ff2, l0_ln2_g, l0_ln2_b, l1_w_in, l1_pool_w, l1_pool_scale, l1_w_out, l1_ln1_g, l1_ln1_b, l1_w_ff1, l1_w_ff2, l1_ln2_g, l1_ln2_b, l2_w_in, l2_q_gain, l2_k_gain, l2_w_out, l2_ln1_g, l2_ln1_b, l2_w_ff1, l2_w_ff2, l2_ln2_g, l2_ln2_b, l3_w_in, l3_pool_w, l3_pool_scale, l3_w_out, l3_ln1_g, l3_ln1_b, l3_w_ff1, l3_w_ff2, l3_ln2_g, l3_ln2_b):
    layers = [
        (l0_w_in, l0_q_gain, l0_k_gain, l0_w_out, l0_ln1_g, l0_ln1_b, l0_w_ff1, l0_w_ff2, l0_ln2_g, l0_ln2_b),
        (l1_w_in, l1_pool_w, l1_pool_scale, l1_w_out, l1_ln1_g, l1_ln1_b, l1_w_ff1, l1_w_ff2, l1_ln2_g, l1_ln2_b),
        (l2_w_in, l2_q_gain, l2_k_gain, l2_w_out, l2_ln1_g, l2_ln1_b, l2_w_ff1, l2_w_ff2, l2_ln2_g, l2_ln2_b),
        (l3_w_in, l3_pool_w, l3_pool_scale, l3_w_out, l3_ln1_g, l3_ln1_b, l3_w_ff1, l3_w_ff2, l3_ln2_g, l3_ln2_b),
    ]
    b, s_len, d = x.shape
    assert b == 1 and d == D_MODEL

    mem_kv = _matmul(mem[0].astype(BF16), w_mem_kv.astype(BF16),
                     col0=0, ncols=2 * MEM_WIDTH, tn=MEM_WIDTH, out_dtype=BF16)
    cos_t, sin_t = _rope_tables(s_len)
    q_scale = HEAD_DIM ** -0.5 * math.log2(math.e)
    cos_q, sin_q = cos_t * q_scale, sin_t * q_scale

    hb = x[0].astype(BF16)
    w_in_slot, w_out_slot, w_ff1_slot, w_ff2_slot = 0, 3, 6, 7
    cast = {}

    def weight(layer, slot):
        if (layer, slot) in cast:
            return cast[(layer, slot)]
        return layers[layer][slot].astype(BF16)

    for i in range(DEPTH):
        _, pa, pb, _, ln1_g, ln1_b, _, _, ln2_g, ln2_b = layers[i]
        w_in = weight(i, w_in_slot)
        if i % 2 == 0:
            qt = _qk_proj(hb, w_in, _vec(pa), cos_q, sin_q, col0=0, ncols=MIXER_WIDTH, tn=512,
                          transpose_out=True)
            k = _qk_proj(hb, w_in, _vec(pb), cos_t, sin_t, col0=MIXER_WIDTH, ncols=KV_WIDTH, tn=KV_WIDTH)
            vt = _v_proj(hb, w_in, col0=MIXER_WIDTH + KV_WIDTH, tk=min(ATTN_KEY_CHUNK, s_len // 2))
            qm = _matmul(hb, w_in, col0=MIXER_WIDTH + 2 * KV_WIDTH, ncols=MEM_WIDTH, tn=512, out_dtype=BF16)
            wanted = [(i, w_out_slot), (i, w_ff1_slot), (i, w_ff2_slot)]
            wanted += [(i + 1, s) for s in (w_in_slot, w_out_slot, w_ff1_slot, w_ff2_slot)]
            if i + 2 < DEPTH:
                wanted.append((i + 2, w_in_slot))
            y, *done = lax.cond(_score_bound(pa, pb, q_scale) <= SCORE_BOUND_LIMIT,
                                functools.partial(_attention, bounded=True),
                                functools.partial(_attention, bounded=False),
                                qt, k, vt, *[layers[l][s] for l, s in wanted])
            cast.update(zip(wanted, done))
            y = _mem_attention(qm, 0, mem_kv, y)
        else:
            proj = _matmul(hb, w_in, col0=0, ncols=D_MODEL, tn=1024, out_dtype=BF16)
            y = _pool_mix(proj, pa.astype(BF16), _vec(pb))
            y = _mem_attention(proj, MIXER_WIDTH // MEM_WIDTH, mem_kv, y)
        hb = _proj_ln(y, weight(i, w_out_slot), hb, _vec(ln1_g), _vec(ln1_b))
        hb = _mlp_ln(hb, weight(i, w_ff1_slot), weight(i, w_ff2_slot), _vec(ln2_g), _vec(ln2_b),
                     out_dtype=F32 if i == DEPTH - 1 else BF16)
    return hb[None]
```
